```python
import jax, jax.numpy as jnp
from jax import lax
import numpy as np

D_MODEL = 1024
BATCH = 8
SEQ = 2048
DEPTH = 2
DEC_BATCH = 32
DEC_SEQ = 1
PAST_LEN = 16384
PAGE_SIZE = 128

GLA_HEADS = 4
GLA_DK = 64
GLA_DV = 128
GLA_GATE_RANK = 16
GLA_TAU = 16.0
GLA_CHUNK = 64
SB_HEADS = 8
SB_HEAD_DIM = 64
SB_QBLOCK = 128
SB_BIAS_INIT = -8.0
N_MEM = 256
MEM_HEADS = 4
MEM_HEAD_DIM = 128
D_FF = 2816
FFN_CONV = 3
EPS = 1e-6

GLA_QK = GLA_HEADS * GLA_DK
GLA_VW = GLA_HEADS * GLA_DV
SB_W = SB_HEADS * SB_HEAD_DIM
MEM_WIDTH = MEM_HEADS * MEM_HEAD_DIM
TOKEN_WIDTH = 512
MIX_WIDTH = TOKEN_WIDTH + MEM_WIDTH
GLA_IN = 2 * GLA_QK + 2 * GLA_VW + GLA_GATE_RANK + MEM_WIDTH
SB_IN = 3 * SB_W + MEM_WIDTH
N_GLA_LAYERS = (DEPTH + 1) // 2
N_SB_LAYERS = DEPTH // 2

kernel_name = 'hybrid_gla_stickbreak_memxattn_convffn_step'


def rmsnorm(x, g):
    xf = x.astype(jnp.float32)
    y = xf * lax.rsqrt(jnp.mean(xf * xf, axis=-1, keepdims=True) + EPS)
    return (y * g.astype(jnp.float32)).astype(x.dtype)


def gla_chunked(q, k, v, log_a):
    B, L, H, DK = q.shape
    DV = v.shape[-1]
    C = GLA_CHUNK
    N = L // C
    def blk(t):
        return t.astype(jnp.float32).reshape(B, N, C, H, t.shape[-1]).transpose(1, 0, 3, 2, 4)
    qc, kc, vc, ac = blk(q), blk(k), blk(v), blk(log_a)
    b = jnp.cumsum(ac, axis=3)
    b_last = b[:, :, :, -1, :]
    q_dec = qc * jnp.exp(b)
    k_dec = kc * jnp.exp(-b)
    k_to_end = kc * jnp.exp(b_last[:, :, :, None, :] - b)
    causal = jnp.tril(jnp.ones((C, C), dtype=bool))
    att = jnp.where(causal, jnp.einsum('nbhck,nbhsk->nbhcs', q_dec, k_dec), 0.0)
    o_intra = jnp.einsum('nbhcs,nbhsv->nbhcv', att, vc)
    def step(S, inp):
        qd, kt, vv, bl = inp
        o_inter = jnp.einsum('bhck,bhkv->bhcv', qd, S)
        S = jnp.exp(bl)[..., None] * S + jnp.einsum('bhck,bhcv->bhkv', kt, vv)
        return S, o_inter
    S0 = jnp.zeros((B, H, DK, DV), jnp.float32)
    S_fin, o_inter = lax.scan(step, S0, (q_dec, k_to_end, vc, b_last))
    o = (o_intra + o_inter).transpose(1, 0, 3, 2, 4).reshape(B, L, H, DV)
    return o, S_fin


def gla_recurrent(q, k, v, log_a, S0):
    def step(S, inp):
        qt, kt, vt, at = inp
        S = jnp.exp(at)[..., None] * S + kt[..., :, None] * vt[..., None, :]
        return S, jnp.einsum('bhk,bhkv->bhv', qt, S)
    xs = tuple(t.astype(jnp.float32).transpose(1, 0, 2, 3) for t in (q, k, v, log_a))
    S, o = lax.scan(step, S0.astype(jnp.float32), xs)
    return o.transpose(1, 0, 2, 3), S


def gla_mixer(z, w_gate_up, b_gate, out_norm, S0):
    B, L, _ = z.shape
    o1 = 2 * GLA_QK
    o2 = o1 + GLA_VW
    o3 = o2 + GLA_VW
    o4 = o3 + GLA_GATE_RANK
    q = z[..., :GLA_QK].reshape(B, L, GLA_HEADS, GLA_DK) * (GLA_DK ** -0.5)
    k = z[..., GLA_QK:o1].reshape(B, L, GLA_HEADS, GLA_DK)
    v = z[..., o1:o2].reshape(B, L, GLA_HEADS, GLA_DV)
    g = z[..., o2:o3]
    gd = z[..., o3:o4]
    mq = z[..., o4:]
    log_a = (jax.nn.log_sigmoid((gd @ w_gate_up + b_gate).astype(jnp.float32)) / GLA_TAU).reshape(B, L, GLA_HEADS, GLA_DK)
    if S0 is None:
        o, S = gla_chunked(q, k, v, log_a)
    else:
        o, S = gla_recurrent(q, k, v, log_a, S0)
    o = rmsnorm(o.astype(z.dtype), out_norm).reshape(B, L, GLA_VW)
    return o * jax.nn.silu(g), mq, S


def stick_breaking(q, k, v, bias, q_pos0):
    B, Lq, H, D = q.shape
    Lk = k.shape[1]
    blk = min(SB_QBLOCK, Lq)
    nb = -(-Lq // blk)
    q = jnp.pad(q, ((0, 0), (0, nb * blk - Lq), (0, 0), (0, 0)))
    scale = D ** -0.5
    bias_f = bias.astype(jnp.float32)[None, :, None, None]
    outs = []
    for bi in range(nb):
        i0 = bi * blk
        end = min(Lk, q_pos0 + i0 + blk)
        kb, vb = k[:, :end], v[:, :end]
        z = jnp.einsum('bqhd,bkhd->bhqk', q[:, i0:i0 + blk], kb, preferred_element_type=jnp.float32) * scale + bias_f
        qpos = q_pos0 + i0 + jnp.arange(blk)
        mask = jnp.arange(end)[None, :] < qpos[:, None]
        log_fail = jnp.where(mask, jax.nn.log_sigmoid(-z), 0.0)
        after = lax.cumsum(log_fail, axis=3, reverse=True) - log_fail
        w = jnp.where(mask, jnp.exp(jax.nn.log_sigmoid(z) + after), 0.0)
        outs.append(jnp.einsum('bhqk,bkhd->bqhd', w.astype(v.dtype), vb))
    return jnp.concatenate(outs, axis=1)[:, :Lq]


def sb_mixer(z, bias, past_k, past_v):
    B, L, _ = z.shape
    q = z[..., :SB_W].reshape(B, L, SB_HEADS, SB_HEAD_DIM)
    k = z[..., SB_W:2 * SB_W].reshape(B, L, SB_HEADS, SB_HEAD_DIM)
    v = z[..., 2 * SB_W:3 * SB_W].reshape(B, L, SB_HEADS, SB_HEAD_DIM)
    mq = z[..., 3 * SB_W:]
    if past_k is None:
        o = stick_breaking(q, k, v, bias, 0)
    else:
        keys = jnp.concatenate([past_k.astype(k.dtype), k], axis=1)
        vals = jnp.concatenate([past_v.astype(v.dtype), v], axis=1)
        o = stick_breaking(q, keys, vals, bias, past_k.shape[1])
    return o.reshape(B, L, SB_W), mq, k, v


def mem_kv(mem, g_norm, w_kv, k_norm):
    B, N, _ = mem.shape
    kv = rmsnorm(mem, g_norm) @ w_kv
    mk = rmsnorm(kv[..., :MEM_WIDTH].reshape(B, N, MEM_HEADS, MEM_HEAD_DIM), k_norm)
    mv = kv[..., MEM_WIDTH:].reshape(B, N, MEM_HEADS, MEM_HEAD_DIM)
    return mk, mv


def mem_attend(mq, mk, mv):
    s = jnp.einsum('blhd,bmhd->bhlm', mq, mk.astype(mq.dtype), preferred_element_type=jnp.float32) * (MEM_HEAD_DIM ** -0.5)
    p = jax.nn.softmax(s, axis=-1)
    return jnp.einsum('bhlm,bmhd->blhd', p.astype(mq.dtype), mv.astype(mq.dtype))


def conv_ffn(h, conv_prev, w_up, conv_w, conv_b, w_down):
    L = h.shape[1]
    up = h @ w_up
    gate, val = up[..., :D_FF], up[..., D_FF:]
    ext = jnp.concatenate([conv_prev.astype(gate.dtype), gate], axis=1)
    acc = conv_b
    for t in range(FFN_CONV):
        acc = acc + ext[:, t:t + L] * conv_w[t]
    y = (jax.nn.silu(acc) * val) @ w_down
    return y, ext[:, ext.shape[1] - (FFN_CONV - 1):]


def layer_tail(x, tok, mq, mk, mv, q_norm, w_o, g_ffn, conv_prev, w_up, conv_w, conv_b, w_down):
    B, L, _ = x.shape
    mqn = rmsnorm(mq.reshape(B, L, MEM_HEADS, MEM_HEAD_DIM), q_norm)
    m_out = mem_attend(mqn, mk, mv).reshape(B, L, MEM_WIDTH)
    x = x + jnp.concatenate([tok, m_out], axis=-1) @ w_o
    y, conv_new = conv_ffn(rmsnorm(x, g_ffn), conv_prev, w_up, conv_w, conv_b, w_down)
    return x + y, conv_new


def setup_inputs(seed: int = 0) -> dict:
    key = jax.random.key(seed)
    ks = jax.random.split(key, 32)
    f32 = jnp.float32
    n_pages = PAST_LEN // PAGE_SIZE
    n_phys = (DEC_BATCH * n_pages * 5) // 4
    def nrm(k, shape, scale):
        return jax.random.normal(k, shape, f32) * scale
    def gain(k, shape):
        return 1.0 + 0.02 * jax.random.normal(k, shape, f32)
    page_table = jax.random.permutation(ks[0], n_phys)[:DEC_BATCH * n_pages].reshape(DEC_BATCH, n_pages).astype(jnp.int32)
    return {
        'x_prompt': nrm(ks[1], (BATCH, SEQ, D_MODEL), 1.0),
        'x_sample': nrm(ks[2], (DEC_BATCH, DEC_SEQ, D_MODEL), 1.0),
        'state_gla': nrm(ks[3], (N_GLA_LAYERS, DEC_BATCH, GLA_HEADS, GLA_DK, GLA_DV), 1.0),
        'cache_sb_k': nrm(ks[4], (N_SB_LAYERS, n_phys, PAGE_SIZE, SB_HEADS, SB_HEAD_DIM), 1.0),
        'cache_sb_v': nrm(ks[5], (N_SB_LAYERS, n_phys, PAGE_SIZE, SB_HEADS, SB_HEAD_DIM), 1.0),
        'cache_mem_k': nrm(ks[6], (DEPTH, DEC_BATCH, N_MEM, MEM_HEADS, MEM_HEAD_DIM), 1.0),
        'cache_mem_v': nrm(ks[7], (DEPTH, DEC_BATCH, N_MEM, MEM_HEADS, MEM_HEAD_DIM), 1.0),
        'state_ffn_conv': nrm(ks[8], (DEPTH, DEC_BATCH, FFN_CONV - 1, D_FF), 1.0),
        'page_table': page_table,
        'mem_prompt': nrm(ks[9], (BATCH, N_MEM, D_MODEL), 1.0),
        'norm_mix': gain(ks[10], (DEPTH, D_MODEL)),
        'w_in_gla': nrm(ks[11], (N_GLA_LAYERS, D_MODEL, GLA_IN), D_MODEL ** -0.5),
        'w_gate_up': nrm(ks[12], (N_GLA_LAYERS, GLA_GATE_RANK, GLA_QK), GLA_GATE_RANK ** -0.5),
        'b_gate': nrm(ks[13], (N_GLA_LAYERS, GLA_QK), 0.01),
        'gla_out_norm': gain(ks[14], (N_GLA_LAYERS, GLA_DV)),
        'w_in_sb': nrm(ks[15], (N_SB_LAYERS, D_MODEL, SB_IN), D_MODEL ** -0.5),
        'sb_bias': SB_BIAS_INIT + 0.1 * jax.random.normal(ks[26], (N_SB_LAYERS, SB_HEADS), f32),
        'mem_norm': gain(ks[16], (DEPTH, D_MODEL)),
        'w_mem_kv': nrm(ks[17], (DEPTH, D_MODEL, 2 * MEM_WIDTH), D_MODEL ** -0.5),
        'mem_q_norm': gain(ks[18], (DEPTH, MEM_HEAD_DIM)),
        'mem_k_norm': gain(ks[19], (DEPTH, MEM_HEAD_DIM)),
        'w_out': nrm(ks[20], (DEPTH, MIX_WIDTH, D_MODEL), MIX_WIDTH ** -0.5),
        'norm_ffn': gain(ks[21], (DEPTH, D_MODEL)),
        'w_ffn_up': nrm(ks[22], (DEPTH, D_MODEL, 2 * D_FF), D_MODEL ** -0.5),
        'ffn_conv_w': nrm(ks[23], (DEPTH, FFN_CONV, D_FF), FFN_CONV ** -0.5),
        'ffn_conv_b': nrm(ks[24], (DEPTH, D_FF), 0.01),
        'w_ffn_down': nrm(ks[25], (DEPTH, D_FF, D_MODEL), D_FF ** -0.5),
    }


def reference(x_prompt, x_sample, state_gla, cache_sb_k, cache_sb_v, cache_mem_k, cache_mem_v, state_ffn_conv, page_table, mem_prompt, norm_mix, w_in_gla, w_gate_up, b_gate, gla_out_norm, w_in_sb, sb_bias, mem_norm, w_mem_kv, mem_q_norm, mem_k_norm, w_out, norm_ffn, w_ffn_up, ffn_conv_w, ffn_conv_b, w_ffn_down):
    x = x_prompt
    p_gla, p_sbk, p_sbv, p_mk, p_mv, p_conv = [], [], [], [], [], []
    conv0 = jnp.zeros((x.shape[0], FFN_CONV - 1, D_FF), x.dtype)
    for i in range(DEPTH):
        j = i // 2
        h = rmsnorm(x, norm_mix[i])
        if i % 2 == 0:
            tok, mq, S = gla_mixer(h @ w_in_gla[j], w_gate_up[j], b_gate[j], gla_out_norm[j], None)
            p_gla.append(S.astype(x.dtype))
        else:
            tok, mq, k, v = sb_mixer(h @ w_in_sb[j], sb_bias[j], None, None)
            p_sbk.append(k)
            p_sbv.append(v)
        mk, mv = mem_kv(mem_prompt, mem_norm[i], w_mem_kv[i], mem_k_norm[i])
        p_mk.append(mk)
        p_mv.append(mv)
        x, c = layer_tail(x, tok, mq, mk, mv, mem_q_norm[i], w_out[i], norm_ffn[i], conv0, w_ffn_up[i], ffn_conv_w[i], ffn_conv_b[i], w_ffn_down[i])
        p_conv.append(c)
    y_prompt = x

    x = x_sample
    db = x.shape[0]
    n_pages = page_table.shape[1]
    s_gla, s_sbk, s_sbv, s_conv = [], [], [], []
    for i in range(DEPTH):
        j = i // 2
        h = rmsnorm(x, norm_mix[i])
        if i % 2 == 0:
            tok, mq, S = gla_mixer(h @ w_in_gla[j], w_gate_up[j], b_gate[j], gla_out_norm[j], state_gla[j])
            s_gla.append(S.astype(state_gla.dtype))
        else:
            past_k = cache_sb_k[j][page_table].reshape(db, n_pages * cache_sb_k.shape[2], SB_HEADS, SB_HEAD_DIM)
            past_v = cache_sb_v[j][page_table].reshape(db, n_pages * cache_sb_v.shape[2], SB_HEADS, SB_HEAD_DIM)
            tok, mq, k, v = sb_mixer(h @ w_in_sb[j], sb_bias[j], past_k, past_v)
            s_sbk.append(k)
            s_sbv.append(v)
        x, c = layer_tail(x, tok, mq, cache_mem_k[i], cache_mem_v[i], mem_q_norm[i], w_out[i], norm_ffn[i], state_ffn_conv[i], w_ffn_up[i], ffn_conv_w[i], ffn_conv_b[i], w_ffn_down[i])
        s_conv.append(c)
    y_sample = x

    return (y_prompt, y_sample,
            jnp.stack(p_gla), jnp.stack(s_gla),
            jnp.stack(p_sbk), jnp.stack(p_sbv), jnp.stack(s_sbk), jnp.stack(s_sbv),
            jnp.stack(p_mk), jnp.stack(p_mv),
            jnp.stack(p_conv), jnp.stack(s_conv))
```

```python
import functools

import jax
import jax.numpy as jnp
from jax import lax
from jax.experimental import pallas as pl
from jax.experimental.pallas import tpu as pltpu

F32 = jnp.float32
BF16 = jnp.bfloat16

GLA_HEADS = 4
GLA_DK = 64
GLA_DV = 128
GLA_GATE_RANK = 16
GLA_TAU = 16.0
GLA_CHUNK = 64
SB_HEADS = 8
SB_HEAD_DIM = 64
SB_QBLOCK = 128
MEM_HEADS = 4
MEM_HEAD_DIM = 128
D_FF = 2816
FFN_CONV = 3
EPS = 1e-6

GLA_QK = GLA_HEADS * GLA_DK
GLA_VW = GLA_HEADS * GLA_DV
SB_W = SB_HEADS * SB_HEAD_DIM
MEM_WIDTH = MEM_HEADS * MEM_HEAD_DIM

LANES = 128
FFN_CHUNK = 256
PAGES_PER_STEP = 8


def _cparams(semantics, vmem_mib):
    return pltpu.CompilerParams(dimension_semantics=semantics,
                                vmem_limit_bytes=vmem_mib << 20)


def _dot(a, b):
    return jnp.dot(a, b, preferred_element_type=F32)


def _dot_nt(a, b):
    return lax.dot_general(a, b, (((1,), (1,)), ((), ())), preferred_element_type=F32)


def _rms(xf, g):
    ms = jnp.mean(xf * xf, axis=-1, keepdims=True)
    return xf * lax.rsqrt(ms + EPS) * g


def _log_sigmoid_pair(z):
    t = jnp.log1p(jnp.exp(-jnp.abs(z)))
    return jnp.minimum(z, 0.0) - t, jnp.minimum(-z, 0.0) - t


def _silu(x):
    return x * jax.nn.sigmoid(x)


def _split_bf16(x):
    hi = x.astype(BF16)
    lo = (x - hi.astype(F32)).astype(BF16)
    return hi, lo


def _dot_split(x, m):
    hi, lo = _split_bf16(x)
    return _dot(hi, m) + _dot(lo, m)


def _dot_split_left(m, x):
    hi, lo = _split_bf16(x)
    return _dot(m, hi) + _dot(m, lo)


def _proj_kernel(*refs, widths, gla):
    if gla:
        x_ref, g_ref, w_ref, wgd_ref, wgu_ref, bg_ref = refs[:6]
        outs = refs[6:]
    else:
        x_ref, g_ref, w_ref = refs[:3]
        outs = refs[3:]
    h = _rms(x_ref[...], g_ref[...]).astype(BF16)
    z = _dot(h, w_ref[...])
    off = 0
    for o_ref, wd in zip(outs, widths):
        o_ref[...] = z[:, off:off + wd]
        off += wd
    if gla:
        gd = _dot(h, wgd_ref[...])
        pre = _dot(gd.astype(BF16), wgu_ref[...]) + bg_ref[...]
        ls, _ = _log_sigmoid_pair(pre)
        outs[-1][...] = ls * (1.0 / GLA_TAU)


def _proj(x, gain, w, widths, gate=None):
    T, D = x.shape
    tm = min(T, 256)
    const = lambda i: (0, 0)
    in_specs = [pl.BlockSpec((tm, D), lambda i: (i, 0)),
                pl.BlockSpec((1, D), const),
                pl.BlockSpec(w.shape, const)]
    args = [x, gain.reshape(1, D), w]
    out_widths = list(widths)
    if gate is not None:
        wgd, wgu, bg = gate
        in_specs += [pl.BlockSpec(wgd.shape, const), pl.BlockSpec(wgu.shape, const),
                     pl.BlockSpec((1, GLA_QK), const)]
        args += [wgd, wgu, bg.reshape(1, GLA_QK)]
        out_widths.append(GLA_QK)
    return pl.pallas_call(
        functools.partial(_proj_kernel, widths=tuple(widths), gla=gate is not None),
        grid=(T // tm,),
        in_specs=in_specs,
        out_specs=[pl.BlockSpec((tm, wd), lambda i: (i, 0)) for wd in out_widths],
        out_shape=[jax.ShapeDtypeStruct((T, wd), F32) for wd in out_widths],
        compiler_params=_cparams(("parallel",), 40),
        name="proj_gla" if gate is not None else "proj_sb",
    )(*args)


def _gla_prompt_kernel(q_ref, k_ref, la_ref, v_ref, g_ref, on_ref, tok_ref, s_ref, s_scr):
    i = pl.program_id(1)

    @pl.when(i == 0)
    def _():
        s_scr[...] = jnp.zeros_like(s_scr)

    R = q_ref.shape[0]
    C = GLA_CHUNK
    row = lax.broadcasted_iota(jnp.int32, (R, R), 0)
    col = lax.broadcasted_iota(jnp.int32, (R, R), 1)
    same = (row // C) == (col // C)
    causal = jnp.logical_and(same, col <= row)
    ltri = jnp.where(causal, 1.0, 0.0).astype(BF16)
    ones_bd = jnp.where(same, 1.0, 0.0).astype(BF16)

    la = la_ref[...]
    b = _dot_split_left(ltri, la)
    bt = _dot_split_left(ones_bd, la)
    q_dec = q_ref[...] * (GLA_DK ** -0.5) * jnp.exp(b)
    k = k_ref[...]
    k_dec = k * jnp.exp(-b)
    kte_t = (k * jnp.exp(bt - b)).T
    dec_t = jnp.exp(bt).T

    for h in range(GLA_HEADS):
        ks = slice(h * GLA_DK, (h + 1) * GLA_DK)
        vs = slice(h * GLA_DV, (h + 1) * GLA_DV)
        qh = q_dec[:, ks].astype(BF16)
        kh = k_dec[:, ks].astype(BF16)
        att = jnp.where(causal, _dot_nt(qh, kh), 0.0)
        vh = v_ref[:, vs].astype(BF16)
        o = _dot(att.astype(BF16), vh)
        s = s_scr[h]
        inter = []
        for c in range(R // C):
            rs = slice(c * C, (c + 1) * C)
            inter.append(_dot(qh[rs], s.astype(BF16)))
            kv = _dot(kte_t[ks, rs].astype(BF16), vh[rs])
            s = dec_t[ks, c * C:c * C + 1] * s + kv
        s_scr[h] = s
        o = o + jnp.concatenate(inter, axis=0)
        o = _rms(o, on_ref[...])
        tok_ref[:, vs] = o * _silu(g_ref[:, vs])

    @pl.when(i == pl.num_programs(1) - 1)
    def _():
        s_ref[0] = s_scr[...]


def _gla_prompt(q, k, la, v, g, out_norm, batch, seq):
    R = 256
    nb = seq // R
    rows = lambda b, i: (b * nb + i, 0)
    return pl.pallas_call(
        _gla_prompt_kernel,
        grid=(batch, nb),
        in_specs=[pl.BlockSpec((R, GLA_QK), rows), pl.BlockSpec((R, GLA_QK), rows),
                  pl.BlockSpec((R, GLA_QK), rows), pl.BlockSpec((R, GLA_VW), rows),
                  pl.BlockSpec((R, GLA_VW), rows),
                  pl.BlockSpec((1, GLA_DV), lambda b, i: (0, 0))],
        out_specs=[pl.BlockSpec((R, GLA_VW), rows),
                   pl.BlockSpec((1, GLA_HEADS, GLA_DK, GLA_DV), lambda b, i: (b, 0, 0, 0))],
        out_shape=[jax.ShapeDtypeStruct((batch * seq, GLA_VW), F32),
                   jax.ShapeDtypeStruct((batch, GLA_HEADS, GLA_DK, GLA_DV), F32)],
        scratch_shapes=[pltpu.VMEM((GLA_HEADS, GLA_DK, GLA_DV), F32)],
        compiler_params=_cparams(("parallel", "arbitrary"), 32),
        name="gla_prompt",
    )(q, k, la, v, g, out_norm.reshape(1, GLA_DV))


def _gla_step_kernel(qt_ref, kt_ref, lat_ref, v_ref, g_ref, s0_ref, on_ref, tok_ref, sn_ref):
    nb = v_ref.shape[0]
    for i in range(nb):
        for h in range(GLA_HEADS):
            vs = slice(h * GLA_DV, (h + 1) * GLA_DV)
            qc = qt_ref[i, :, h:h + 1] * (GLA_DK ** -0.5)
            kc = kt_ref[i, :, h:h + 1]
            ac = jnp.exp(lat_ref[i, :, h:h + 1])
            s = ac * s0_ref[i, h] + kc * v_ref[i:i + 1, vs]
            sn_ref[i, h] = s
            o = jnp.sum(qc * s, axis=0, keepdims=True)
            o = _rms(o, on_ref[...])
            tok_ref[i:i + 1, vs] = o * _silu(g_ref[i:i + 1, vs])


def _gla_step(q, k, la, v, g, s0, out_norm):
    nseq = q.shape[0]
    nb = 8
    cols = lambda a: a.reshape(nseq, GLA_HEADS, GLA_DK).transpose(0, 2, 1)
    col_spec = pl.BlockSpec((nb, GLA_DK, GLA_HEADS), lambda i: (i, 0, 0))
    row_spec = pl.BlockSpec((nb, GLA_VW), lambda i: (i, 0))
    st_spec = pl.BlockSpec((nb, GLA_HEADS, GLA_DK, GLA_DV), lambda i: (i, 0, 0, 0))
    return pl.pallas_call(
        _gla_step_kernel,
        grid=(nseq // nb,),
        in_specs=[col_spec, col_spec, col_spec, row_spec, row_spec, st_spec,
                  pl.BlockSpec((1, GLA_DV), lambda i: (0, 0))],
        out_specs=[row_spec, st_spec],
        out_shape=[jax.ShapeDtypeStruct((nseq, GLA_VW), F32),
                   jax.ShapeDtypeStruct(s0.shape, F32)],
        compiler_params=_cparams(("parallel",), 32),
        name="gla_step",
    )(cols(q), cols(k), cols(la), v, g, s0, out_norm.reshape(1, GLA_DV))


def _sb_prompt_kernel(bias_ref, q_ref, k_ref, v_ref, o_ref):
    hp = pl.program_id(1)
    i = pl.program_id(2)
    Q = SB_QBLOCK
    lane = lax.broadcasted_iota(jnp.int32, (Q, LANES), 1)
    rowi = lax.broadcasted_iota(jnp.int32, (Q, LANES), 0)
    q = q_ref[...]
    first = lane < SB_HEAD_DIM
    q0 = jnp.where(first, q, 0.0).astype(BF16)
    q1 = jnp.where(first, 0.0, q).astype(BF16)
    uo = jnp.concatenate([jnp.where(rowi > lane, 1.0, 0.0), jnp.ones((Q, LANES), F32)],
                         axis=1).astype(BF16)
    b0 = bias_ref[2 * hp]
    b1 = bias_ref[2 * hp + 1]
    scale = SB_HEAD_DIM ** -0.5

    def body(t, carry):
        c0, c1, a0, a1 = carry
        kb = i - t
        start = pl.multiple_of(kb * Q, Q)
        kk = k_ref[pl.ds(start, Q), :].astype(BF16)
        vv = v_ref[pl.ds(start, Q), :].astype(BF16)
        valid = (kb * Q + lane) < (i * Q + rowi)

        def head(qm, bias, c, a):
            z = _dot_nt(qm, kk) * scale + bias
            ls, lf = _log_sigmoid_pair(z)
            lf = jnp.where(valid, lf, 0.0)
            r = _dot_split(lf, uo)
            after = r[:, :LANES] + c
            w = jnp.where(valid, jnp.exp(ls + after), 0.0)
            return c + r[:, LANES:], a + _dot(w.astype(BF16), vv)

        c0, a0 = head(q0, b0, c0, a0)
        c1, a1 = head(q1, b1, c1, a1)
        return c0, c1, a0, a1

    zeros = jnp.zeros((Q, LANES), F32)
    _, _, a0, a1 = lax.fori_loop(0, i + 1, body, (zeros, zeros, zeros, zeros))
    o_ref[...] = jnp.where(first, a0, a1)


def _sb_prompt(q, k, v, bias, batch, seq):
    Q = SB_QBLOCK
    nq = seq // Q
    npair = SB_W // LANES
    qspec = pl.BlockSpec((Q, LANES), lambda b, hp, i: (b * nq + i, hp))
    kvspec = pl.BlockSpec((seq, LANES), lambda b, hp, i: (b, hp))
    return pl.pallas_call(
        _sb_prompt_kernel,
        grid=(batch, npair, nq),
        in_specs=[pl.BlockSpec(memory_space=pltpu.SMEM), qspec, kvspec, kvspec],
        out_specs=qspec,
        out_shape=jax.ShapeDtypeStruct((batch * seq, SB_W), F32),
        compiler_params=_cparams(("parallel", "parallel", "arbitrary"), 32),
        name="sb_prompt",
    )(bias, q, k, v)


def _sb_decode_kernel(pt_ref, bias_ref, q_ref, *rest):
    NP = PAGES_PER_STEP
    k_refs = rest[:NP]
    v_refs = rest[NP:2 * NP]
    o_ref = rest[2 * NP]
    acc_scr, carry_scr = rest[2 * NP + 1:]
    s = pl.program_id(1)

    @pl.when(s == 0)
    def _():
        acc_scr[...] = jnp.zeros_like(acc_scr)
        carry_scr[...] = jnp.zeros_like(carry_scr)

    H = SB_HEADS
    P = k_refs[0].shape[1]
    sub = lax.broadcasted_iota(jnp.int32, (H, SB_W), 0)
    lane = lax.broadcasted_iota(jnp.int32, (H, SB_W), 1)
    diag = (lane // SB_HEAD_DIM) == sub
    qbd = jnp.where(diag, jnp.broadcast_to(q_ref[0], (H, SB_W)), 0.0).astype(BF16)
    hrow = lax.broadcasted_iota(jnp.int32, (H, P), 0)
    bias = jnp.zeros((H, P), F32)
    for h in range(H):
        bias = jnp.where(hrow == h, bias_ref[h], bias)
    rowi = lax.broadcasted_iota(jnp.int32, (P, P), 0)
    coli = lax.broadcasted_iota(jnp.int32, (P, P), 1)
    uo = jnp.concatenate([jnp.where(rowi > coli, 1.0, 0.0), jnp.ones((P, P), F32)],
                         axis=1).astype(BF16)

    z = jnp.concatenate(
        [_dot_nt(qbd, k_refs[p][0].astype(BF16)) * (SB_HEAD_DIM ** -0.5) + bias
         for p in range(NP)], axis=0)
    ls, lf = _log_sigmoid_pair(z)
    r = _dot_split(lf, uo)
    run = carry_scr[...]
    acc = acc_scr[...]
    for p in reversed(range(NP)):
        rs = slice(p * H, (p + 1) * H)
        w = jnp.exp(ls[rs] + r[rs, :P] + run)
        run = run + r[rs, P:]
        acc = acc + _dot(w.astype(BF16), v_refs[p][0].astype(BF16))
    carry_scr[...] = run
    acc_scr[...] = acc

    @pl.when(s == pl.num_programs(1) - 1)
    def _():
        o_ref[0] = jnp.sum(jnp.where(diag, acc, 0.0), axis=0, keepdims=True)


def _sb_decode(q, cache_k, cache_v, page_table, bias):
    nseq = q.shape[0]
    n_pages = page_table.shape[1]
    n_phys, P = cache_k.shape[0], cache_k.shape[1]
    NP = PAGES_PER_STEP
    nsteps = n_pages // NP
    ck = cache_k.reshape(n_phys, P, SB_W)
    cv = cache_v.reshape(n_phys, P, SB_W)

    def page_spec(p):
        return pl.BlockSpec((1, P, SB_W),
                            lambda b, s, pt: (pt[b, (nsteps - 1 - s) * NP + p], 0, 0))

    qspec = pl.BlockSpec((1, 1, SB_W), lambda b, s, pt: (b, 0, 0))
    grid_spec = pltpu.PrefetchScalarGridSpec(
        num_scalar_prefetch=1,
        grid=(nseq, nsteps),
        in_specs=[pl.BlockSpec(memory_space=pltpu.SMEM), qspec]
        + [page_spec(p) for p in range(NP)] * 2,
        out_specs=qspec,
        scratch_shapes=[pltpu.VMEM((SB_HEADS, SB_W), F32), pltpu.VMEM((SB_HEADS, P), F32)],
    )
    out = pl.pallas_call(
        _sb_decode_kernel,
        grid_spec=grid_spec,
        out_shape=jax.ShapeDtypeStruct((nseq, 1, SB_W), F32),
        compiler_params=_cparams(("parallel", "arbitrary"), 32),
        name="sb_decode",
    )(page_table, bias, q.reshape(nseq, 1, SB_W), *([ck] * NP), *([cv] * NP))
    return out.reshape(nseq, SB_W)


def _mem_kv_kernel(m_ref, g_ref, w_ref, kn_ref, mk_ref, mv_ref):
    h = _rms(m_ref[...], g_ref[...]).astype(BF16)
    kv = _dot(h, w_ref[...])
    for hd in range(MEM_HEADS):
        hs = slice(hd * MEM_HEAD_DIM, (hd + 1) * MEM_HEAD_DIM)
        mk_ref[:, hs] = _rms(kv[:, hs], kn_ref[...])
    mv_ref[...] = kv[:, MEM_WIDTH:]


def _mem_kv(mem, gain, w, k_norm):
    T, D = mem.shape
    tm = 256
    const = lambda i: (0, 0)
    return pl.pallas_call(
        _mem_kv_kernel,
        grid=(T // tm,),
        in_specs=[pl.BlockSpec((tm, D), lambda i: (i, 0)), pl.BlockSpec((1, D), const),
                  pl.BlockSpec(w.shape, const), pl.BlockSpec((1, MEM_HEAD_DIM), const)],
        out_specs=[pl.BlockSpec((tm, MEM_WIDTH), lambda i: (i, 0))] * 2,
        out_shape=[jax.ShapeDtypeStruct((T, MEM_WIDTH), F32)] * 2,
        compiler_params=_cparams(("parallel",), 32),
        name="mem_kv",
    )(mem, gain.reshape(1, D), w, k_norm.reshape(1, MEM_HEAD_DIM))


def _mem_attn_kernel(q_ref, mk_ref, mv_ref, qn_ref, o_ref):
    tm = q_ref.shape[1]
    q = q_ref[0]
    if tm < 8:
        q = jnp.broadcast_to(q, (8, MEM_WIDTH))
    for hd in range(MEM_HEADS):
        hs = slice(hd * MEM_HEAD_DIM, (hd + 1) * MEM_HEAD_DIM)
        qh = _rms(q[:, hs], qn_ref[...]).astype(BF16)
        s = _dot_nt(qh, mk_ref[0, :, hs].astype(BF16)) * (MEM_HEAD_DIM ** -0.5)
        e = jnp.exp(s - jnp.max(s, axis=-1, keepdims=True))
        p = e / jnp.sum(e, axis=-1, keepdims=True)
        o = _dot(p.astype(BF16), mv_ref[0, :, hs].astype(BF16))
        o_ref[0, :, hs] = o[:tm]


def _mem_attn(mq, mk, mv, q_norm):
    nb, L, _ = mq.shape
    n_mem = mk.shape[1]
    tm = min(L, 256)
    qspec = pl.BlockSpec((1, tm, MEM_WIDTH), lambda b, i: (b, i, 0))
    kvspec = pl.BlockSpec((1, n_mem, MEM_WIDTH), lambda b, i: (b, 0, 0))
    return pl.pallas_call(
        _mem_attn_kernel,
        grid=(nb, L // tm),
        in_specs=[qspec, kvspec, kvspec, pl.BlockSpec((1, MEM_HEAD_DIM), lambda b, i: (0, 0))],
        out_specs=qspec,
        out_shape=jax.ShapeDtypeStruct(mq.shape, F32),
        compiler_params=_cparams(("parallel", "arbitrary"), 32),
        name="mem_attn",
    )(mq, mk, mv, q_norm.reshape(1, MEM_HEAD_DIM))


def _out_proj_kernel(x_ref, tok_ref, mo_ref, w_ref, g_ref, x1_ref, h_ref):
    tw = tok_ref.shape[1]
    y = _dot(tok_ref[...].astype(BF16), w_ref[:tw, :]) + _dot(mo_ref[...].astype(BF16), w_ref[tw:, :])
    x1 = x_ref[...] + y
    x1_ref[...] = x1
    h_ref[...] = _rms(x1, g_ref[...]).astype(BF16)


def _out_proj(x, tok, mo, w, gain):
    T, D = x.shape
    tm = min(T, 256)
    const = lambda i: (0, 0)
    rows = lambda i: (i, 0)
    return pl.pallas_call(
        _out_proj_kernel,
        grid=(T // tm,),
        in_specs=[pl.BlockSpec((tm, D), rows), pl.BlockSpec((tm, tok.shape[1]), rows),
                  pl.BlockSpec((tm, mo.shape[1]), rows), pl.BlockSpec(w.shape, const),
                  pl.BlockSpec((1, D), const)],
        out_specs=[pl.BlockSpec((tm, D), rows)] * 2,
        out_shape=[jax.ShapeDtypeStruct((T, D), F32), jax.ShapeDtypeStruct((T, D), BF16)],
        compiler_params=_cparams(("parallel",), 32),
        name="out_proj",
    )(x, tok, mo, w, gain.reshape(1, D))


def _ffn_prompt_kernel(h_ref, x_ref, wu_ref, cw_ref, cb_ref, wd_ref, o_ref, cs_ref, carry_scr):
    i = pl.program_id(1)

    @pl.when(i == 0)
    def _():
        carry_scr[...] = jnp.zeros_like(carry_scr)

    tm = h_ref.shape[0]
    FC = FFN_CHUNK
    h = h_ref[...]
    rowi = lax.broadcasted_iota(jnp.int32, (tm, FC), 0)
    acc = x_ref[...]
    for c in range(D_FF // FC):
        cs = slice(c * FC, (c + 1) * FC)
        gate = _dot(h, wu_ref[:, cs])
        val = _dot(h, wu_ref[:, D_FF + c * FC:D_FF + (c + 1) * FC])
        p0 = carry_scr[0:1, cs]
        p1 = carry_scr[1:2, cs]
        g1 = jnp.where(rowi == 0, p1, pltpu.roll(gate, 1, 0))
        g2 = jnp.where(rowi == 0, p0, jnp.where(rowi == 1, p1, pltpu.roll(gate, 2, 0)))
        a = cb_ref[:, cs] + g2 * cw_ref[0:1, cs] + g1 * cw_ref[1:2, cs] + gate * cw_ref[2:3, cs]
        act = (_silu(a) * val).astype(BF16)
        acc = acc + _dot(act, wd_ref[cs, :])
        last = gate[tm - (FFN_CONV - 1):, :]
        carry_scr[0:FFN_CONV - 1, cs] = last
        cs_ref[0, :, cs] = last
    o_ref[...] = acc


def _ffn_prompt(h, x1, w_up, conv_w, conv_b, w_down, batch, seq):
    T, D = x1.shape
    tm = 256
    nb = seq // tm
    rows = lambda b, i: (b * nb + i, 0)
    const = lambda b, i: (0, 0)
    once = pl.Buffered(1)
    return pl.pallas_call(
        _ffn_prompt_kernel,
        grid=(batch, nb),
        in_specs=[pl.BlockSpec((tm, D), rows), pl.BlockSpec((tm, D), rows),
                  pl.BlockSpec(w_up.shape, const, pipeline_mode=once),
                  pl.BlockSpec(conv_w.shape, const), pl.BlockSpec((1, D_FF), const),
                  pl.BlockSpec(w_down.shape, const, pipeline_mode=once)],
        out_specs=[pl.BlockSpec((tm, D), rows),
                   pl.BlockSpec((1, FFN_CONV - 1, D_FF), lambda b, i: (b, 0, 0))],
        out_shape=[jax.ShapeDtypeStruct((T, D), F32),
                   jax.ShapeDtypeStruct((batch, FFN_CONV - 1, D_FF), F32)],
        scratch_shapes=[pltpu.VMEM((8, D_FF), F32)],
        compiler_params=_cparams(("parallel", "arbitrary"), 48),
        name="ffn_prompt",
    )(h, x1, w_up, conv_w, conv_b.reshape(1, D_FF), w_down)


def _ffn_step_kernel(h_ref, x_ref, wg_ref, wv_ref, cw_ref, cb_ref, wd_ref, p0_ref, p1_ref,
                     o_ref, gate_ref):
    @pl.when(pl.program_id(0) == 0)
    def _():
        o_ref[...] = x_ref[...]

    h = h_ref[...]
    gate = _dot(h, wg_ref[...])
    val = _dot(h, wv_ref[...])
    a = (cb_ref[...] + p0_ref[...] * cw_ref[0:1, :] + p1_ref[...] * cw_ref[1:2, :]
         + gate * cw_ref[2:3, :])
    o_ref[...] += _dot((_silu(a) * val).astype(BF16), wd_ref[...])
    gate_ref[...] = gate


def _ffn_step(h, x1, w_up, conv_w, conv_b, w_down, prev):
    T, D = x1.shape
    FC = FFN_CHUNK
    nch = D_FF // FC
    const = lambda c: (0, 0)
    chunk = lambda c: (0, c)
    out, gate = pl.pallas_call(
        _ffn_step_kernel,
        grid=(nch,),
        in_specs=[pl.BlockSpec((T, D), const), pl.BlockSpec((T, D), const),
                  pl.BlockSpec((D, FC), chunk), pl.BlockSpec((D, FC), lambda c: (0, nch + c)),
                  pl.BlockSpec((FFN_CONV, FC), chunk), pl.BlockSpec((1, FC), chunk),
                  pl.BlockSpec((FC, D), lambda c: (c, 0)),
                  pl.BlockSpec((T, FC), chunk), pl.BlockSpec((T, FC), chunk)],
        out_specs=[pl.BlockSpec((T, D), const), pl.BlockSpec((T, FC), chunk)],
        out_shape=[jax.ShapeDtypeStruct((T, D), F32), jax.ShapeDtypeStruct((T, D_FF), F32)],
        compiler_params=_cparams(("arbitrary",), 32),
        name="ffn_step",
    )(h, x1, w_up, w_up, conv_w, conv_b.reshape(1, D_FF), w_down, prev[:, 0], prev[:, 1])
    return out, jnp.stack([prev[:, 1], gate], axis=1)


def kernel(x_prompt, x_sample, state_gla, cache_sb_k, cache_sb_v, cache_mem_k, cache_mem_v, state_ffn_conv, page_table, mem_prompt, norm_mix, w_in_gla, w_gate_up, b_gate, gla_out_norm, w_in_sb, sb_bias, mem_norm, w_mem_kv, mem_q_norm, mem_k_norm, w_out, norm_ffn, w_ffn_up, ffn_conv_w, ffn_conv_b, w_ffn_down):
    batch, seq, d_model = x_prompt.shape
    nseq = x_sample.shape[0]
    n_mem = mem_prompt.shape[1]
    depth = norm_mix.shape[0]
    gd0 = 2 * GLA_QK + 2 * GLA_VW
    gla_widths = (GLA_QK, GLA_QK, GLA_VW, GLA_VW, MEM_WIDTH)
    sb_widths = (SB_W, SB_W, SB_W, MEM_WIDTH)

    def gla_weights(j):
        w = w_in_gla[j]
        w_main = jnp.concatenate([w[:, :gd0], w[:, gd0 + GLA_GATE_RANK:]], axis=1).astype(BF16)
        w_gd = jnp.pad(w[:, gd0:gd0 + GLA_GATE_RANK], ((0, 0), (0, LANES - GLA_GATE_RANK))).astype(BF16)
        w_gu = jnp.pad(w_gate_up[j], ((0, LANES - GLA_GATE_RANK), (0, 0))).astype(BF16)
        return w_main, (w_gd, w_gu, b_gate[j])

    w_out_b = w_out.astype(BF16)
    w_up_b = w_ffn_up.astype(BF16)
    w_down_b = w_ffn_down.astype(BF16)
    w_kv_b = w_mem_kv.astype(BF16)
    w_sb_b = w_in_sb.astype(BF16)
    gla_w = [gla_weights(j) for j in range(w_in_gla.shape[0])]

    x = x_prompt.reshape(batch * seq, d_model)
    mem = mem_prompt.reshape(batch * n_mem, d_model)
    p_gla, p_sbk, p_sbv, p_mk, p_mv, p_conv = [], [], [], [], [], []
    for i in range(depth):
        j = i // 2
        if i % 2 == 0:
            w_main, gate = gla_w[j]
            q, k, v, g, mq, la = _proj(x, norm_mix[i], w_main, gla_widths, gate)
            tok, s_fin = _gla_prompt(q, k, la, v, g, gla_out_norm[j], batch, seq)
            p_gla.append(s_fin)
        else:
            q, k, v, mq = _proj(x, norm_mix[i], w_sb_b[j], sb_widths)
            tok = _sb_prompt(q, k, v, sb_bias[j], batch, seq)
            p_sbk.append(k.reshape(batch, seq, SB_HEADS, SB_HEAD_DIM))
            p_sbv.append(v.reshape(batch, seq, SB_HEADS, SB_HEAD_DIM))
        mk, mv = _mem_kv(mem, mem_norm[i], w_kv_b[i], mem_k_norm[i])
        p_mk.append(mk.reshape(batch, n_mem, MEM_HEADS, MEM_HEAD_DIM))
        p_mv.append(mv.reshape(batch, n_mem, MEM_HEADS, MEM_HEAD_DIM))
        mo = _mem_attn(mq.reshape(batch, seq, MEM_WIDTH), mk.reshape(batch, n_mem, MEM_WIDTH),
                       mv.reshape(batch, n_mem, MEM_WIDTH), mem_q_norm[i])
        x1, h2 = _out_proj(x, tok, mo.reshape(batch * seq, MEM_WIDTH), w_out_b[i], norm_ffn[i])
        x, conv = _ffn_prompt(h2, x1, w_up_b[i], ffn_conv_w[i], ffn_conv_b[i], w_down_b[i], batch, seq)
        p_conv.append(conv)
    y_prompt = x.reshape(batch, seq, d_model)

    dec_seq = x_sample.shape[1]
    x = x_sample.reshape(nseq * dec_seq, d_model)
    s_gla, s_sbk, s_sbv, s_conv = [], [], [], []
    for i in range(depth):
        j = i // 2
        if i % 2 == 0:
            w_main, gate = gla_w[j]
            q, k, v, g, mq, la = _proj(x, norm_mix[i], w_main, gla_widths, gate)
            tok, s_new = _gla_step(q, k, la, v, g, state_gla[j], gla_out_norm[j])
            s_gla.append(s_new)
        else:
            q, k, v, mq = _proj(x, norm_mix[i], w_sb_b[j], sb_widths)
            tok = _sb_decode(q, cache_sb_k[j], cache_sb_v[j], page_table, sb_bias[j])
            s_sbk.append(k.reshape(nseq, dec_seq, SB_HEADS, SB_HEAD_DIM))
            s_sbv.append(v.reshape(nseq, dec_seq, SB_HEADS, SB_HEAD_DIM))
        mo = _mem_attn(mq.reshape(nseq, dec_seq, MEM_WIDTH),
                       cache_mem_k[i].reshape(nseq, n_mem, MEM_WIDTH),
                       cache_mem_v[i].reshape(nseq, n_mem, MEM_WIDTH), mem_q_norm[i])
        x1, h2 = _out_proj(x, tok, mo.reshape(nseq * dec_seq, MEM_WIDTH), w_out_b[i], norm_ffn[i])
        x, conv = _ffn_step(h2, x1, w_up_b[i], ffn_conv_w[i], ffn_conv_b[i], w_down_b[i],
                            state_ffn_conv[i])
        s_conv.append(conv)
    y_sample = x.reshape(nseq, dec_seq, d_model)

    return (y_prompt, y_sample,
            jnp.stack(p_gla), jnp.stack(s_gla),
            jnp.stack(p_sbk), jnp.stack(p_sbv), jnp.stack(s_sbk), jnp.stack(s_sbv),
            jnp.stack(p_mk), jnp.stack(p_mv),
            jnp.stack(p_conv), jnp.stack(s_conv))
```

```python
import functools

import jax
import jax.numpy as jnp
from jax import lax
from jax.experimental import pallas as pl
from jax.experimental.pallas import tpu as pltpu

F32 = jnp.float32
BF16 = jnp.bfloat16

GLA_HEADS = 4
GLA_DK = 64
GLA_DV = 128
GLA_GATE_RANK = 16
GLA_TAU = 16.0
GLA_CHUNK = 64
SB_HEADS = 8
SB_HEAD_DIM = 64
SB_QBLOCK = 128
MEM_HEADS = 4
MEM_HEAD_DIM = 128
D_FF = 2816
FFN_CONV = 3
EPS = 1e-6

GLA_QK = GLA_HEADS * GLA_DK
GLA_VW = GLA_HEADS * GLA_DV
SB_W = SB_HEADS * SB_HEAD_DIM
MEM_WIDTH = MEM_HEADS * MEM_HEAD_DIM

LOG2E = 1.4426950408889634
LANES = 128
FFN_CHUNK = 256
PAGES_PER_STEP = 8


def _cparams(semantics, vmem_mib):
    return pltpu.CompilerParams(dimension_semantics=semantics,
                                vmem_limit_bytes=vmem_mib << 20)


def _dot(a, b):
    return jnp.dot(a, b, preferred_element_type=F32)


def _dot_nt(a, b):
    return lax.dot_general(a, b, (((1,), (1,)), ((), ())), preferred_element_type=F32)


def _rms(xf, g):
    ms = jnp.mean(xf * xf, axis=-1, keepdims=True)
    return xf * lax.rsqrt(ms + EPS) * g


def _log_sigmoid_pair(z):
    t = jnp.log1p(jnp.exp(-jnp.abs(z)))
    return jnp.minimum(z, 0.0) - t, jnp.minimum(-z, 0.0) - t


def _softplus(z):
    return jnp.maximum(z, 0.0) + jnp.log(1.0 + jnp.exp2(jnp.abs(z) * (-LOG2E)))


def _silu(x):
    return x * jax.nn.sigmoid(x)


def _split_bf16(x):
    hi = x.astype(BF16)
    lo = (x - hi.astype(F32)).astype(BF16)
    return hi, lo


def _dot_split(x, m):
    hi, lo = _split_bf16(x)
    return _dot(hi, m) + _dot(lo, m)


def _dot_split_left(m, x):
    hi, lo = _split_bf16(x)
    return _dot(m, hi) + _dot(m, lo)


def _proj_kernel(*refs, widths, gla):
    if gla:
        x_ref, g_ref, w_ref, wgd_ref, wgu_ref, bg_ref = refs[:6]
        outs = refs[6:]
    else:
        x_ref, g_ref, w_ref = refs[:3]
        outs = refs[3:]
    h = _rms(x_ref[...], g_ref[...]).astype(BF16)
    z = _dot(h, w_ref[...])
    off = 0
    for o_ref, wd in zip(outs, widths):
        o_ref[...] = z[:, off:off + wd]
        off += wd
    if gla:
        gd = _dot(h, wgd_ref[...])
        pre = _dot(gd.astype(BF16), wgu_ref[...]) + bg_ref[...]
        ls, _ = _log_sigmoid_pair(pre)
        outs[-1][...] = ls * (1.0 / GLA_TAU)


def _proj(x, gain, w, widths, gate=None):
    T, D = x.shape
    tm = min(T, 512)
    const = lambda i: (0, 0)
    in_specs = [pl.BlockSpec((tm, D), lambda i: (i, 0)),
                pl.BlockSpec((1, D), const),
                pl.BlockSpec(w.shape, const)]
    args = [x, gain.reshape(1, D), w]
    out_widths = list(widths)
    if gate is not None:
        wgd, wgu, bg = gate
        in_specs += [pl.BlockSpec(wgd.shape, const), pl.BlockSpec(wgu.shape, const),
                     pl.BlockSpec((1, GLA_QK), const)]
        args += [wgd, wgu, bg.reshape(1, GLA_QK)]
        out_widths.append(GLA_QK)
    return pl.pallas_call(
        functools.partial(_proj_kernel, widths=tuple(widths), gla=gate is not None),
        grid=(T // tm,),
        in_specs=in_specs,
        out_specs=[pl.BlockSpec((tm, wd), lambda i: (i, 0)) for wd in out_widths],
        out_shape=[jax.ShapeDtypeStruct((T, wd), F32) for wd in out_widths],
        compiler_params=_cparams(("parallel",), 40),
        name="proj_gla" if gate is not None else "proj_sb",
    )(*args)


def _gla_prompt_kernel(q_ref, k_ref, la_ref, v_ref, g_ref, on_ref, tok_ref, s_ref, s_scr):
    i = pl.program_id(1)

    @pl.when(i == 0)
    def _():
        s_scr[...] = jnp.zeros_like(s_scr)

    R = q_ref.shape[0]
    C = GLA_CHUNK
    row = lax.broadcasted_iota(jnp.int32, (R, R), 0)
    col = lax.broadcasted_iota(jnp.int32, (R, R), 1)
    same = (row // C) == (col // C)
    causal = jnp.logical_and(same, col <= row)
    ltri = jnp.where(causal, 1.0, 0.0).astype(BF16)
    ones_bd = jnp.where(same, 1.0, 0.0).astype(BF16)

    la = la_ref[...]
    b = _dot_split_left(ltri, la)
    bt = _dot_split_left(ones_bd, la)
    q_dec = q_ref[...] * (GLA_DK ** -0.5) * jnp.exp(b)
    k = k_ref[...]
    k_dec = k * jnp.exp(-b)
    kte_t = (k * jnp.exp(bt - b)).T
    dec_t = jnp.exp(bt).T

    for h in range(GLA_HEADS):
        ks = slice(h * GLA_DK, (h + 1) * GLA_DK)
        vs = slice(h * GLA_DV, (h + 1) * GLA_DV)
        qh = q_dec[:, ks].astype(BF16)
        kh = k_dec[:, ks].astype(BF16)
        att = jnp.where(causal, _dot_nt(qh, kh), 0.0)
        vh = v_ref[:, vs].astype(BF16)
        o = _dot(att.astype(BF16), vh)
        s = s_scr[h]
        inter = []
        for c in range(R // C):
            rs = slice(c * C, (c + 1) * C)
            inter.append(_dot(qh[rs], s.astype(BF16)))
            kv = _dot(kte_t[ks, rs].astype(BF16), vh[rs])
            s = dec_t[ks, c * C:c * C + 1] * s + kv
        s_scr[h] = s
        o = o + jnp.concatenate(inter, axis=0)
        o = _rms(o, on_ref[...])
        tok_ref[:, vs] = o * _silu(g_ref[:, vs])

    @pl.when(i == pl.num_programs(1) - 1)
    def _():
        s_ref[0] = s_scr[...]


def _gla_prompt(q, k, la, v, g, out_norm, batch, seq):
    R = 256
    nb = seq // R
    rows = lambda b, i: (b * nb + i, 0)
    return pl.pallas_call(
        _gla_prompt_kernel,
        grid=(batch, nb),
        in_specs=[pl.BlockSpec((R, GLA_QK), rows), pl.BlockSpec((R, GLA_QK), rows),
                  pl.BlockSpec((R, GLA_QK), rows), pl.BlockSpec((R, GLA_VW), rows),
                  pl.BlockSpec((R, GLA_VW), rows),
                  pl.BlockSpec((1, GLA_DV), lambda b, i: (0, 0))],
        out_specs=[pl.BlockSpec((R, GLA_VW), rows),
                   pl.BlockSpec((1, GLA_HEADS, GLA_DK, GLA_DV), lambda b, i: (b, 0, 0, 0))],
        out_shape=[jax.ShapeDtypeStruct((batch * seq, GLA_VW), F32),
                   jax.ShapeDtypeStruct((batch, GLA_HEADS, GLA_DK, GLA_DV), F32)],
        scratch_shapes=[pltpu.VMEM((GLA_HEADS, GLA_DK, GLA_DV), F32)],
        compiler_params=_cparams(("parallel", "arbitrary"), 32),
        name="gla_prompt",
    )(q, k, la, v, g, out_norm.reshape(1, GLA_DV))


def _gla_step_kernel(qt_ref, kt_ref, lat_ref, v_ref, g_ref, s0_ref, on_ref, tok_ref, sn_ref):
    nb = v_ref.shape[0]
    for i in range(nb):
        for h in range(GLA_HEADS):
            vs = slice(h * GLA_DV, (h + 1) * GLA_DV)
            qc = qt_ref[i, :, h:h + 1] * (GLA_DK ** -0.5)
            kc = kt_ref[i, :, h:h + 1]
            ac = jnp.exp(lat_ref[i, :, h:h + 1])
            s = ac * s0_ref[i, h] + kc * v_ref[i:i + 1, vs]
            sn_ref[i, h] = s
            o = jnp.sum(qc * s, axis=0, keepdims=True)
            o = _rms(o, on_ref[...])
            tok_ref[i:i + 1, vs] = o * _silu(g_ref[i:i + 1, vs])


def _gla_step(q, k, la, v, g, s0, out_norm):
    nseq = q.shape[0]
    nb = 8
    cols = lambda a: a.reshape(nseq, GLA_HEADS, GLA_DK).transpose(0, 2, 1)
    col_spec = pl.BlockSpec((nb, GLA_DK, GLA_HEADS), lambda i: (i, 0, 0))
    row_spec = pl.BlockSpec((nb, GLA_VW), lambda i: (i, 0))
    st_spec = pl.BlockSpec((nb, GLA_HEADS, GLA_DK, GLA_DV), lambda i: (i, 0, 0, 0))
    return pl.pallas_call(
        _gla_step_kernel,
        grid=(nseq // nb,),
        in_specs=[col_spec, col_spec, col_spec, row_spec, row_spec, st_spec,
                  pl.BlockSpec((1, GLA_DV), lambda i: (0, 0))],
        out_specs=[row_spec, st_spec],
        out_shape=[jax.ShapeDtypeStruct((nseq, GLA_VW), F32),
                   jax.ShapeDtypeStruct(s0.shape, F32)],
        compiler_params=_cparams(("parallel",), 32),
        name="gla_step",
    )(cols(q), cols(k), cols(la), v, g, s0, out_norm.reshape(1, GLA_DV))


def _sb_prompt_kernel(bias_ref, q_ref, k_ref, v_ref, o_ref, c_scr, acc_scr):
    hp = pl.program_id(1)
    i = pl.program_id(2)
    QB = q_ref.shape[0]
    lane = lax.broadcasted_iota(jnp.int32, (QB, LANES), 1)
    first = lane < SB_HEAD_DIM
    q = q_ref[...] * (SB_HEAD_DIM ** -0.5)
    qq = jnp.concatenate([jnp.where(first, q, 0.0), jnp.where(first, 0.0, q)], axis=0).astype(BF16)
    rowi = lax.broadcasted_iota(jnp.int32, (2 * QB, QB), 0)
    coli = lax.broadcasted_iota(jnp.int32, (2 * QB, QB), 1)
    r1 = lax.broadcasted_iota(jnp.int32, (QB, QB), 0)
    c1 = lax.broadcasted_iota(jnp.int32, (QB, QB), 1)
    later = jnp.where(r1 > c1, 1.0, 0.0).astype(BF16)
    strict = coli < jnp.bitwise_and(rowi, QB - 1)
    b0 = bias_ref[2 * hp]
    b1 = bias_ref[2 * hp + 1]

    def block(kb, c, masked):
        start = pl.multiple_of(kb * QB, QB)
        kk = k_ref[pl.ds(start, QB), :].astype(BF16)
        vv = v_ref[pl.ds(start, QB), :].astype(BF16)
        z = _dot_nt(qq, kk)
        z = jnp.concatenate([z[:QB] + b0, z[QB:] + b1], axis=0)
        nlf = _softplus(z)
        if masked:
            nlf = jnp.where(strict, nlf, 0.0)
        after = _dot(nlf.astype(BF16), later) + jnp.concatenate([c] * (QB // LANES), axis=1)
        w = jnp.exp((z - nlf) - after)
        if masked:
            w = jnp.where(strict, w, 0.0)
        return c + jnp.sum(nlf, axis=1, keepdims=True), _dot(w.astype(BF16), vv)

    c, pv = block(i, jnp.zeros((2 * QB, LANES), F32), True)
    c_scr[...] = c
    acc_scr[...] = pv

    @pl.loop(0, i // 2)
    def _(t):
        kb = i - 1 - 2 * t
        c1, pv1 = block(kb, c_scr[...], False)
        c2, pv2 = block(kb - 1, c1, False)
        c_scr[...] = c2
        acc_scr[...] += pv1 + pv2

    @pl.when(i % 2 == 1)
    def _():
        _, pv1 = block(0, c_scr[...], False)
        acc_scr[...] += pv1

    o_ref[...] = jnp.where(first, acc_scr[:QB], acc_scr[QB:])


def _sb_prompt(q, k, v, bias, batch, seq):
    QB = 2 * SB_QBLOCK
    nq = seq // QB
    npair = SB_W // LANES
    qspec = pl.BlockSpec((QB, LANES), lambda b, hp, i: (b * nq + i, hp))
    kvspec = pl.BlockSpec((seq, LANES), lambda b, hp, i: (b, hp))
    return pl.pallas_call(
        _sb_prompt_kernel,
        grid=(batch, npair, nq),
        in_specs=[pl.BlockSpec(memory_space=pltpu.SMEM), qspec, kvspec, kvspec],
        out_specs=qspec,
        out_shape=jax.ShapeDtypeStruct((batch * seq, SB_W), F32),
        scratch_shapes=[pltpu.VMEM((2 * QB, LANES), F32), pltpu.VMEM((2 * QB, LANES), F32)],
        compiler_params=_cparams(("parallel", "parallel", "arbitrary"), 32),
        name="sb_prompt",
    )(bias, q, k, v)


def _sb_decode_kernel(pt_ref, bias_ref, q_ref, *rest):
    NP = PAGES_PER_STEP
    k_refs = rest[:NP]
    v_refs = rest[NP:2 * NP]
    o_ref = rest[2 * NP]
    acc_scr, carry_scr = rest[2 * NP + 1:]
    s = pl.program_id(1)

    @pl.when(s == 0)
    def _():
        acc_scr[...] = jnp.zeros_like(acc_scr)
        carry_scr[...] = jnp.zeros_like(carry_scr)

    H = SB_HEADS
    P = k_refs[0].shape[2]
    PH = P * H
    NCH = PH // LANES
    qm = (q_ref[0] * (SB_HEAD_DIM ** -0.5)).astype(BF16)
    sub = lax.broadcasted_iota(jnp.int32, (H, LANES), 0)
    lane = lax.broadcasted_iota(jnp.int32, (H, LANES), 1)
    own = (lane % H) == sub
    bias = jnp.zeros((H, LANES), F32)
    for h in range(H):
        bias = jnp.where(sub == h, bias_ref[h], bias)
    rowi = lax.broadcasted_iota(jnp.int32, (LANES, LANES), 0)
    coli = lax.broadcasted_iota(jnp.int32, (LANES, LANES), 1)
    later = jnp.where(rowi > coli, 1.0, 0.0).astype(BF16)

    z, nlf = [], []
    for p in range(NP):
        kf = k_refs[p][0, 0].reshape(PH, SB_HEAD_DIM).astype(BF16)
        zp = _dot_nt(qm, kf)
        for c in range(NCH):
            zc = zp[:, c * LANES:(c + 1) * LANES] + bias
            z.append(zc)
            n = _softplus(zc)
            nlf.append(jnp.where(own, n, 0.0))
    nl = jnp.concatenate(nlf, axis=0)
    after = _dot(nl.astype(BF16), later)
    run = carry_scr[...]
    acc = acc_scr[...]
    for p in reversed(range(NP)):
        ws = [None] * NCH
        for c in reversed(range(NCH)):
            idx = p * NCH + c
            rs = slice(idx * H, (idx + 1) * H)
            w = jnp.exp((z[idx] - nlf[idx]) - (after[rs] + run))
            ws[c] = jnp.where(own, w, 0.0)
            run = run + jnp.sum(nlf[idx], axis=1, keepdims=True)
        vf = v_refs[p][0, 0].reshape(PH, SB_HEAD_DIM).astype(BF16)
        acc = acc + _dot(jnp.concatenate(ws, axis=1).astype(BF16), vf)
    carry_scr[...] = run
    acc_scr[...] = acc

    @pl.when(s == pl.num_programs(1) - 1)
    def _():
        o_ref[0] = acc


def _sb_decode(q, cache_k, cache_v, layer, page_table, bias):
    nseq = q.shape[0]
    n_pages = page_table.shape[1]
    P = cache_k.shape[2]
    NP = PAGES_PER_STEP
    nsteps = n_pages // NP

    def page_spec(p):
        return pl.BlockSpec((1, 1, P, SB_HEADS, SB_HEAD_DIM),
                            lambda b, s, pt: (layer, pt[b, (nsteps - 1 - s) * NP + p], 0, 0, 0))

    qspec = pl.BlockSpec((1, SB_HEADS, SB_HEAD_DIM), lambda b, s, pt: (b, 0, 0))
    grid_spec = pltpu.PrefetchScalarGridSpec(
        num_scalar_prefetch=1,
        grid=(nseq, nsteps),
        in_specs=[pl.BlockSpec(memory_space=pltpu.SMEM), qspec]
        + [page_spec(p) for p in range(NP)] * 2,
        out_specs=qspec,
        scratch_shapes=[pltpu.VMEM((SB_HEADS, SB_HEAD_DIM), F32),
                        pltpu.VMEM((SB_HEADS, LANES), F32)],
    )
    out = pl.pallas_call(
        _sb_decode_kernel,
        grid_spec=grid_spec,
        out_shape=jax.ShapeDtypeStruct((nseq, SB_HEADS, SB_HEAD_DIM), F32),
        compiler_params=_cparams(("parallel", "arbitrary"), 48),
        name="sb_decode",
    )(page_table, bias, q.reshape(nseq, SB_HEADS, SB_HEAD_DIM),
      *([cache_k] * NP), *([cache_v] * NP))
    return out.reshape(nseq, SB_W)


def _mem_kv_kernel(m_ref, g_ref, w_ref, kn_ref, mk_ref, mv_ref):
    h = _rms(m_ref[...], g_ref[...]).astype(BF16)
    kv = _dot(h, w_ref[...])
    for hd in range(MEM_HEADS):
        hs = slice(hd * MEM_HEAD_DIM, (hd + 1) * MEM_HEAD_DIM)
        mk_ref[:, hs] = _rms(kv[:, hs], kn_ref[...])
    mv_ref[...] = kv[:, MEM_WIDTH:]


def _mem_kv(mem, gain, w, k_norm):
    T, D = mem.shape
    tm = 256
    const = lambda i: (0, 0)
    return pl.pallas_call(
        _mem_kv_kernel,
        grid=(T // tm,),
        in_specs=[pl.BlockSpec((tm, D), lambda i: (i, 0)), pl.BlockSpec((1, D), const),
                  pl.BlockSpec(w.shape, const), pl.BlockSpec((1, MEM_HEAD_DIM), const)],
        out_specs=[pl.BlockSpec((tm, MEM_WIDTH), lambda i: (i, 0))] * 2,
        out_shape=[jax.ShapeDtypeStruct((T, MEM_WIDTH), F32)] * 2,
        compiler_params=_cparams(("parallel",), 32),
        name="mem_kv",
    )(mem, gain.reshape(1, D), w, k_norm.reshape(1, MEM_HEAD_DIM))


def _mem_attn_kernel(q_ref, mk_ref, mv_ref, qn_ref, o_ref):
    tm = q_ref.shape[1]
    q = q_ref[0]
    if tm < 8:
        q = jnp.broadcast_to(q, (8, MEM_WIDTH))
    for hd in range(MEM_HEADS):
        hs = slice(hd * MEM_HEAD_DIM, (hd + 1) * MEM_HEAD_DIM)
        qh = _rms(q[:, hs], qn_ref[...]).astype(BF16)
        s = _dot_nt(qh, mk_ref[0, :, hs].astype(BF16)) * (MEM_HEAD_DIM ** -0.5)
        e = jnp.exp(s - jnp.max(s, axis=-1, keepdims=True))
        p = e / jnp.sum(e, axis=-1, keepdims=True)
        o = _dot(p.astype(BF16), mv_ref[0, :, hs].astype(BF16))
        o_ref[0, :, hs] = o[:tm]


def _mem_attn(mq, mk, mv, q_norm):
    nb, L, _ = mq.shape
    n_mem = mk.shape[1]
    tm = min(L, 256)
    qspec = pl.BlockSpec((1, tm, MEM_WIDTH), lambda b, i: (b, i, 0))
    kvspec = pl.BlockSpec((1, n_mem, MEM_WIDTH), lambda b, i: (b, 0, 0))
    return pl.pallas_call(
        _mem_attn_kernel,
        grid=(nb, L // tm),
        in_specs=[qspec, kvspec, kvspec, pl.BlockSpec((1, MEM_HEAD_DIM), lambda b, i: (0, 0))],
        out_specs=qspec,
        out_shape=jax.ShapeDtypeStruct(mq.shape, F32),
        compiler_params=_cparams(("parallel", "arbitrary"), 32),
        name="mem_attn",
    )(mq, mk, mv, q_norm.reshape(1, MEM_HEAD_DIM))


def _out_proj_kernel(x_ref, tok_ref, mo_ref, w_ref, g_ref, x1_ref, h_ref):
    tw = tok_ref.shape[1]
    y = _dot(tok_ref[...].astype(BF16), w_ref[:tw, :]) + _dot(mo_ref[...].astype(BF16), w_ref[tw:, :])
    x1 = x_ref[...] + y
    x1_ref[...] = x1
    h_ref[...] = _rms(x1, g_ref[...]).astype(BF16)


def _out_proj(x, tok, mo, w, gain):
    T, D = x.shape
    tm = min(T, 256)
    const = lambda i: (0, 0)
    rows = lambda i: (i, 0)
    return pl.pallas_call(
        _out_proj_kernel,
        grid=(T // tm,),
        in_specs=[pl.BlockSpec((tm, D), rows), pl.BlockSpec((tm, tok.shape[1]), rows),
                  pl.BlockSpec((tm, mo.shape[1]), rows), pl.BlockSpec(w.shape, const),
                  pl.BlockSpec((1, D), const)],
        out_specs=[pl.BlockSpec((tm, D), rows)] * 2,
        out_shape=[jax.ShapeDtypeStruct((T, D), F32), jax.ShapeDtypeStruct((T, D), BF16)],
        compiler_params=_cparams(("parallel",), 32),
        name="out_proj",
    )(x, tok, mo, w, gain.reshape(1, D))


def _ffn_prompt_kernel(h_ref, x_ref, wu_ref, cw_ref, cb_ref, wd_ref, o_ref, cs_ref, carry_scr,
                       act_scr):
    i = pl.program_id(1)

    @pl.when(i == 0)
    def _():
        carry_scr[...] = jnp.zeros_like(carry_scr)

    tm = h_ref.shape[0]
    FC = FFN_CHUNK
    h = h_ref[...]
    rowi = lax.broadcasted_iota(jnp.int32, (tm, FC), 0)
    for c in range(D_FF // FC):
        cs = slice(c * FC, (c + 1) * FC)
        gate = _dot(h, wu_ref[:, cs])
        val = _dot(h, wu_ref[:, D_FF + c * FC:D_FF + (c + 1) * FC])
        p0 = carry_scr[0:1, cs]
        p1 = carry_scr[1:2, cs]
        g1 = jnp.where(rowi == 0, p1, pltpu.roll(gate, 1, 0))
        g2 = jnp.where(rowi == 0, p0, jnp.where(rowi == 1, p1, pltpu.roll(gate, 2, 0)))
        a = cb_ref[:, cs] + g2 * cw_ref[0:1, cs] + g1 * cw_ref[1:2, cs] + gate * cw_ref[2:3, cs]
        act_scr[:, cs] = (_silu(a) * val).astype(BF16)
        last = gate[tm - (FFN_CONV - 1):, :]
        carry_scr[0:FFN_CONV - 1, cs] = last
        cs_ref[0, :, cs] = last
    o_ref[...] = x_ref[...] + _dot(act_scr[...], wd_ref[...])


def _ffn_prompt(h, x1, w_up, conv_w, conv_b, w_down, batch, seq):
    T, D = x1.shape
    tm = 512
    nb = seq // tm
    rows = lambda b, i: (b * nb + i, 0)
    const = lambda b, i: (0, 0)
    once = pl.Buffered(1)
    return pl.pallas_call(
        _ffn_prompt_kernel,
        grid=(batch, nb),
        in_specs=[pl.BlockSpec((tm, D), rows), pl.BlockSpec((tm, D), rows),
                  pl.BlockSpec(w_up.shape, const, pipeline_mode=once),
                  pl.BlockSpec(conv_w.shape, const), pl.BlockSpec((1, D_FF), const),
                  pl.BlockSpec(w_down.shape, const, pipeline_mode=once)],
        out_specs=[pl.BlockSpec((tm, D), rows),
                   pl.BlockSpec((1, FFN_CONV - 1, D_FF), lambda b, i: (b, 0, 0))],
        out_shape=[jax.ShapeDtypeStruct((T, D), F32),
                   jax.ShapeDtypeStruct((batch, FFN_CONV - 1, D_FF), F32)],
        scratch_shapes=[pltpu.VMEM((8, D_FF), F32), pltpu.VMEM((tm, D_FF), BF16)],
        compiler_params=_cparams(("parallel", "arbitrary"), 48),
        name="ffn_prompt",
    )(h, x1, w_up, conv_w, conv_b.reshape(1, D_FF), w_down)


def _ffn_step_kernel(h_ref, x_ref, wg_ref, wv_ref, cw_ref, cb_ref, wd_ref, p0_ref, p1_ref,
                     o_ref, gate_ref):
    @pl.when(pl.program_id(0) == 0)
    def _():
        o_ref[...] = x_ref[...]

    h = h_ref[...]
    gate = _dot(h, wg_ref[...])
    val = _dot(h, wv_ref[...])
    a = (cb_ref[...] + p0_ref[...] * cw_ref[0:1, :] + p1_ref[...] * cw_ref[1:2, :]
         + gate * cw_ref[2:3, :])
    o_ref[...] += _dot((_silu(a) * val).astype(BF16), wd_ref[...])
    gate_ref[...] = gate


def _ffn_step(h, x1, w_up, conv_w, conv_b, w_down, prev):
    T, D = x1.shape
    FC = FFN_CHUNK
    nch = D_FF // FC
    const = lambda c: (0, 0)
    chunk = lambda c: (0, c)
    out, gate = pl.pallas_call(
        _ffn_step_kernel,
        grid=(nch,),
        in_specs=[pl.BlockSpec((T, D), const), pl.BlockSpec((T, D), const),
                  pl.BlockSpec((D, FC), chunk), pl.BlockSpec((D, FC), lambda c: (0, nch + c)),
                  pl.BlockSpec((FFN_CONV, FC), chunk), pl.BlockSpec((1, FC), chunk),
                  pl.BlockSpec((FC, D), lambda c: (c, 0)),
                  pl.BlockSpec((T, FC), chunk), pl.BlockSpec((T, FC), chunk)],
        out_specs=[pl.BlockSpec((T, D), const), pl.BlockSpec((T, FC), chunk)],
        out_shape=[jax.ShapeDtypeStruct((T, D), F32), jax.ShapeDtypeStruct((T, D_FF), F32)],
        compiler_params=_cparams(("arbitrary",), 32),
        name="ffn_step",
    )(h, x1, w_up, w_up, conv_w, conv_b.reshape(1, D_FF), w_down, prev[:, 0], prev[:, 1])
    return out, jnp.stack([prev[:, 1], gate], axis=1)


def kernel(x_prompt, x_sample, state_gla, cache_sb_k, cache_sb_v, cache_mem_k, cache_mem_v, state_ffn_conv, page_table, mem_prompt, norm_mix, w_in_gla, w_gate_up, b_gate, gla_out_norm, w_in_sb, sb_bias, mem_norm, w_mem_kv, mem_q_norm, mem_k_norm, w_out, norm_ffn, w_ffn_up, ffn_conv_w, ffn_conv_b, w_ffn_down):
    batch, seq, d_model = x_prompt.shape
    nseq = x_sample.shape[0]
    n_mem = mem_prompt.shape[1]
    depth = norm_mix.shape[0]
    gd0 = 2 * GLA_QK + 2 * GLA_VW
    gla_widths = (GLA_QK, GLA_QK, GLA_VW, GLA_VW, MEM_WIDTH)
    sb_widths = (SB_W, SB_W, SB_W, MEM_WIDTH)

    def gla_weights(j):
        w = w_in_gla[j]
        w_main = jnp.concatenate([w[:, :gd0], w[:, gd0 + GLA_GATE_RANK:]], axis=1).astype(BF16)
        w_gd = jnp.pad(w[:, gd0:gd0 + GLA_GATE_RANK], ((0, 0), (0, LANES - GLA_GATE_RANK))).astype(BF16)
        w_gu = jnp.pad(w_gate_up[j], ((0, LANES - GLA_GATE_RANK), (0, 0))).astype(BF16)
        return w_main, (w_gd, w_gu, b_gate[j])

    w_out_b = w_out.astype(BF16)
    w_up_b = w_ffn_up.astype(BF16)
    w_down_b = w_ffn_down.astype(BF16)
    w_kv_b = w_mem_kv.astype(BF16)
    w_sb_b = w_in_sb.astype(BF16)
    gla_w = [gla_weights(j) for j in range(w_in_gla.shape[0])]

    x = x_prompt.reshape(batch * seq, d_model)
    mem = mem_prompt.reshape(batch * n_mem, d_model)
    p_gla, p_sbk, p_sbv, p_mk, p_mv, p_conv = [], [], [], [], [], []
    for i in range(depth):
        j = i // 2
        if i % 2 == 0:
            w_main, gate = gla_w[j]
            q, k, v, g, mq, la = _proj(x, norm_mix[i], w_main, gla_widths, gate)
            tok, s_fin = _gla_prompt(q, k, la, v, g, gla_out_norm[j], batch, seq)
            p_gla.append(s_fin)
        else:
            q, k, v, mq = _proj(x, norm_mix[i], w_sb_b[j], sb_widths)
            tok = _sb_prompt(q, k, v, sb_bias[j], batch, seq)
            p_sbk.append(k.reshape(batch, seq, SB_HEADS, SB_HEAD_DIM))
            p_sbv.append(v.reshape(batch, seq, SB_HEADS, SB_HEAD_DIM))
        mk, mv = _mem_kv(mem, mem_norm[i], w_kv_b[i], mem_k_norm[i])
        p_mk.append(mk.reshape(batch, n_mem, MEM_HEADS, MEM_HEAD_DIM))
        p_mv.append(mv.reshape(batch, n_mem, MEM_HEADS, MEM_HEAD_DIM))
        mo = _mem_attn(mq.reshape(batch, seq, MEM_WIDTH), mk.reshape(batch, n_mem, MEM_WIDTH),
                       mv.reshape(batch, n_mem, MEM_WIDTH), mem_q_norm[i])
        x1, h2 = _out_proj(x, tok, mo.reshape(batch * seq, MEM_WIDTH), w_out_b[i], norm_ffn[i])
        x, conv = _ffn_prompt(h2, x1, w_up_b[i], ffn_conv_w[i], ffn_conv_b[i], w_down_b[i], batch, seq)
        p_conv.append(conv)
    y_prompt = x.reshape(batch, seq, d_model)

    dec_seq = x_sample.shape[1]
    x = x_sample.reshape(nseq * dec_seq, d_model)
    s_gla, s_sbk, s_sbv, s_conv = [], [], [], []
    for i in range(depth):
        j = i // 2
        if i % 2 == 0:
            w_main, gate = gla_w[j]
            q, k, v, g, mq, la = _proj(x, norm_mix[i], w_main, gla_widths, gate)
            tok, s_new = _gla_step(q, k, la, v, g, state_gla[j], gla_out_norm[j])
            s_gla.append(s_new)
        else:
            q, k, v, mq = _proj(x, norm_mix[i], w_sb_b[j], sb_widths)
            tok = _sb_decode(q, cache_sb_k, cache_sb_v, j, page_table, sb_bias[j])
            s_sbk.append(k.reshape(nseq, dec_seq, SB_HEADS, SB_HEAD_DIM))
            s_sbv.append(v.reshape(nseq, dec_seq, SB_HEADS, SB_HEAD_DIM))
        mo = _mem_attn(mq.reshape(nseq, dec_seq, MEM_WIDTH),
                       cache_mem_k[i].reshape(nseq, n_mem, MEM_WIDTH),
                       cache_mem_v[i].reshape(nseq, n_mem, MEM_WIDTH), mem_q_norm[i])
        x1, h2 = _out_proj(x, tok, mo.reshape(nseq * dec_seq, MEM_WIDTH), w_out_b[i], norm_ffn[i])
        x, conv = _ffn_step(h2, x1, w_up_b[i], ffn_conv_w[i], ffn_conv_b[i], w_down_b[i],
                            state_ffn_conv[i])
        s_conv.append(conv)
    y_sample = x.reshape(nseq, dec_seq, d_model)

    return (y_prompt, y_sample,
            jnp.stack(p_gla), jnp.stack(s_gla),
            jnp.stack(p_sbk), jnp.stack(p_sbv), jnp.stack(s_sbk), jnp.stack(s_sbv),
            jnp.stack(p_mk), jnp.stack(p_mv),
            jnp.stack(p_conv), jnp.stack(s_conv))
```

```python
import functools

import jax
import jax.numpy as jnp
from jax import lax
from jax.experimental import pallas as pl
from jax.experimental.pallas import tpu as pltpu

F32 = jnp.float32
BF16 = jnp.bfloat16

GLA_HEADS = 4
GLA_DK = 64
GLA_DV = 128
GLA_GATE_RANK = 16
GLA_TAU = 16.0
GLA_CHUNK = 64
SB_HEADS = 8
SB_HEAD_DIM = 64
SB_QBLOCK = 128
MEM_HEADS = 4
MEM_HEAD_DIM = 128
D_FF = 2816
FFN_CONV = 3
EPS = 1e-6

GLA_QK = GLA_HEADS * GLA_DK
GLA_VW = GLA_HEADS * GLA_DV
SB_W = SB_HEADS * SB_HEAD_DIM
MEM_WIDTH = MEM_HEADS * MEM_HEAD_DIM

LOG2E = 1.4426950408889634
LANES = 128
FFN_CHUNK = 256
PAGES_PER_STEP = 8


def _cparams(semantics, vmem_mib):
    return pltpu.CompilerParams(dimension_semantics=semantics,
                                vmem_limit_bytes=vmem_mib << 20)


def _dot(a, b):
    return jnp.dot(a, b, preferred_element_type=F32)


def _dot_nt(a, b):
    return lax.dot_general(a, b, (((1,), (1,)), ((), ())), preferred_element_type=F32)


def _rms(xf, g):
    ms = jnp.mean(xf * xf, axis=-1, keepdims=True)
    return xf * lax.rsqrt(ms + EPS) * g


def _log_sigmoid_pair(z):
    t = jnp.log1p(jnp.exp(-jnp.abs(z)))
    return jnp.minimum(z, 0.0) - t, jnp.minimum(-z, 0.0) - t


def _softplus(z):
    return jnp.maximum(z, 0.0) + jnp.log(1.0 + jnp.exp2(jnp.abs(z) * (-LOG2E)))


def _silu(x):
    return x * jax.nn.sigmoid(x)


def _split_bf16(x):
    hi = x.astype(BF16)
    lo = (x - hi.astype(F32)).astype(BF16)
    return hi, lo


def _dot_split(x, m):
    hi, lo = _split_bf16(x)
    return _dot(hi, m) + _dot(lo, m)


def _dot_split_left(m, x):
    hi, lo = _split_bf16(x)
    return _dot(m, hi) + _dot(m, lo)


def _proj_kernel(*refs, widths, gla):
    if gla:
        x_ref, g_ref, w_ref, wgd_ref, wgu_ref, bg_ref = refs[:6]
        outs = refs[6:]
    else:
        x_ref, g_ref, w_ref = refs[:3]
        outs = refs[3:]
    h = _rms(x_ref[...], g_ref[...]).astype(BF16)
    z = _dot(h, w_ref[...])
    off = 0
    for o_ref, wd in zip(outs, widths):
        o_ref[...] = z[:, off:off + wd]
        off += wd
    if gla:
        gd = _dot(h, wgd_ref[...])
        pre = _dot(gd.astype(BF16), wgu_ref[...]) + bg_ref[...]
        ls, _ = _log_sigmoid_pair(pre)
        outs[-1][...] = ls * (1.0 / GLA_TAU)


def _proj(x, gain, w, widths, gate=None):
    T, D = x.shape
    tm = min(T, 512)
    const = lambda i: (0, 0)
    in_specs = [pl.BlockSpec((tm, D), lambda i: (i, 0)),
                pl.BlockSpec((1, D), const),
                pl.BlockSpec(w.shape, const)]
    args = [x, gain.reshape(1, D), w]
    out_widths = list(widths)
    if gate is not None:
        wgd, wgu, bg = gate
        in_specs += [pl.BlockSpec(wgd.shape, const), pl.BlockSpec(wgu.shape, const),
                     pl.BlockSpec((1, GLA_QK), const)]
        args += [wgd, wgu, bg.reshape(1, GLA_QK)]
        out_widths.append(GLA_QK)
    return pl.pallas_call(
        functools.partial(_proj_kernel, widths=tuple(widths), gla=gate is not None),
        grid=(T // tm,),
        in_specs=in_specs,
        out_specs=[pl.BlockSpec((tm, wd), lambda i: (i, 0)) for wd in out_widths],
        out_shape=[jax.ShapeDtypeStruct((T, wd), F32) for wd in out_widths],
        compiler_params=_cparams(("parallel",), 40),
        name="proj_gla" if gate is not None else "proj_sb",
    )(*args)


def _proj_sb_prompt_kernel(x_ref, g_ref, w_ref, q_ref, kt_ref, v_ref, vt_ref, mq_ref):
    h = _rms(x_ref[...], g_ref[...]).astype(BF16)
    z = _dot(h, w_ref[...])
    q_ref[...] = z[:, :SB_W]
    k = z[:, SB_W:2 * SB_W]
    v = z[:, 2 * SB_W:3 * SB_W]
    kt_ref[0] = k.T
    v_ref[...] = v
    vt_ref[0] = v.T
    mq_ref[...] = z[:, 3 * SB_W:]


def _proj_sb_prompt(x, gain, w, batch, seq):
    T, D = x.shape
    tm = 512
    nb = seq // tm
    const = lambda i: (0, 0)
    rows = lambda i: (i, 0)
    cols = lambda i: (i // nb, 0, i % nb)
    row_spec = pl.BlockSpec((tm, SB_W), rows)
    col_spec = pl.BlockSpec((1, SB_W, tm), cols)
    row_shape = jax.ShapeDtypeStruct((T, SB_W), F32)
    col_shape = jax.ShapeDtypeStruct((batch, SB_W, seq), F32)
    return pl.pallas_call(
        _proj_sb_prompt_kernel,
        grid=(T // tm,),
        in_specs=[pl.BlockSpec((tm, D), rows), pl.BlockSpec((1, D), const),
                  pl.BlockSpec(w.shape, const)],
        out_specs=[row_spec, col_spec, row_spec, col_spec, row_spec],
        out_shape=[row_shape, col_shape, row_shape, col_shape, row_shape],
        compiler_params=_cparams(("parallel",), 48),
        name="proj_sb_prompt",
    )(x, gain.reshape(1, D), w)


def _gla_prompt_kernel(q_ref, k_ref, la_ref, v_ref, g_ref, on_ref, tok_ref, s_ref, s_scr):
    i = pl.program_id(1)

    @pl.when(i == 0)
    def _():
        s_scr[...] = jnp.zeros_like(s_scr)

    R = q_ref.shape[0]
    C = GLA_CHUNK
    row = lax.broadcasted_iota(jnp.int32, (R, R), 0)
    col = lax.broadcasted_iota(jnp.int32, (R, R), 1)
    same = (row // C) == (col // C)
    causal = jnp.logical_and(same, col <= row)
    ltri = jnp.where(causal, 1.0, 0.0).astype(BF16)
    ones_bd = jnp.where(same, 1.0, 0.0).astype(BF16)

    la = la_ref[...]
    b = _dot_split_left(ltri, la)
    bt = _dot_split_left(ones_bd, la)
    q_dec = q_ref[...] * (GLA_DK ** -0.5) * jnp.exp(b)
    k = k_ref[...]
    k_dec = k * jnp.exp(-b)
    kte_t = (k * jnp.exp(bt - b)).T
    dec_t = jnp.exp(bt).T

    for h in range(GLA_HEADS):
        ks = slice(h * GLA_DK, (h + 1) * GLA_DK)
        vs = slice(h * GLA_DV, (h + 1) * GLA_DV)
        qh = q_dec[:, ks].astype(BF16)
        kh = k_dec[:, ks].astype(BF16)
        att = jnp.where(causal, _dot_nt(qh, kh), 0.0)
        vh = v_ref[:, vs].astype(BF16)
        o = _dot(att.astype(BF16), vh)
        s = s_scr[h]
        inter = []
        for c in range(R // C):
            rs = slice(c * C, (c + 1) * C)
            inter.append(_dot(qh[rs], s.astype(BF16)))
            kv = _dot(kte_t[ks, rs].astype(BF16), vh[rs])
            s = dec_t[ks, c * C:c * C + 1] * s + kv
        s_scr[h] = s
        o = o + jnp.concatenate(inter, axis=0)
        o = _rms(o, on_ref[...])
        tok_ref[:, vs] = o * _silu(g_ref[:, vs])

    @pl.when(i == pl.num_programs(1) - 1)
    def _():
        s_ref[0] = s_scr[...]


def _gla_prompt(q, k, la, v, g, out_norm, batch, seq):
    R = 256
    nb = seq // R
    rows = lambda b, i: (b * nb + i, 0)
    return pl.pallas_call(
        _gla_prompt_kernel,
        grid=(batch, nb),
        in_specs=[pl.BlockSpec((R, GLA_QK), rows), pl.BlockSpec((R, GLA_QK), rows),
                  pl.BlockSpec((R, GLA_QK), rows), pl.BlockSpec((R, GLA_VW), rows),
                  pl.BlockSpec((R, GLA_VW), rows),
                  pl.BlockSpec((1, GLA_DV), lambda b, i: (0, 0))],
        out_specs=[pl.BlockSpec((R, GLA_VW), rows),
                   pl.BlockSpec((1, GLA_HEADS, GLA_DK, GLA_DV), lambda b, i: (b, 0, 0, 0))],
        out_shape=[jax.ShapeDtypeStruct((batch * seq, GLA_VW), F32),
                   jax.ShapeDtypeStruct((batch, GLA_HEADS, GLA_DK, GLA_DV), F32)],
        scratch_shapes=[pltpu.VMEM((GLA_HEADS, GLA_DK, GLA_DV), F32)],
        compiler_params=_cparams(("parallel", "arbitrary"), 32),
        name="gla_prompt",
    )(q, k, la, v, g, out_norm.reshape(1, GLA_DV))


def _gla_step_kernel(qt_ref, kt_ref, lat_ref, v_ref, g_ref, s0_ref, on_ref, tok_ref, sn_ref):
    nb = v_ref.shape[0]
    for i in range(nb):
        for h in range(GLA_HEADS):
            vs = slice(h * GLA_DV, (h + 1) * GLA_DV)
            qc = qt_ref[i, :, h:h + 1] * (GLA_DK ** -0.5)
            kc = kt_ref[i, :, h:h + 1]
            ac = jnp.exp(lat_ref[i, :, h:h + 1])
            s = ac * s0_ref[i, h] + kc * v_ref[i:i + 1, vs]
            sn_ref[i, h] = s
            o = jnp.sum(qc * s, axis=0, keepdims=True)
            o = _rms(o, on_ref[...])
            tok_ref[i:i + 1, vs] = o * _silu(g_ref[i:i + 1, vs])


def _gla_step(q, k, la, v, g, s0, out_norm):
    nseq = q.shape[0]
    nb = 8
    cols = lambda a: a.reshape(nseq, GLA_HEADS, GLA_DK).transpose(0, 2, 1)
    col_spec = pl.BlockSpec((nb, GLA_DK, GLA_HEADS), lambda i: (i, 0, 0))
    row_spec = pl.BlockSpec((nb, GLA_VW), lambda i: (i, 0))
    st_spec = pl.BlockSpec((nb, GLA_HEADS, GLA_DK, GLA_DV), lambda i: (i, 0, 0, 0))
    return pl.pallas_call(
        _gla_step_kernel,
        grid=(nseq // nb,),
        in_specs=[col_spec, col_spec, col_spec, row_spec, row_spec, st_spec,
                  pl.BlockSpec((1, GLA_DV), lambda i: (0, 0))],
        out_specs=[row_spec, st_spec],
        out_shape=[jax.ShapeDtypeStruct((nseq, GLA_VW), F32),
                   jax.ShapeDtypeStruct(s0.shape, F32)],
        compiler_params=_cparams(("parallel",), 32),
        name="gla_step",
    )(cols(q), cols(k), cols(la), v, g, s0, out_norm.reshape(1, GLA_DV))


def _sb_prompt_kernel(bias_ref, q_ref, kt_ref, v_ref, o_ref, c_scr, acc_scr):
    hp = pl.program_id(1)
    i = pl.program_id(2)
    QB = q_ref.shape[0]
    lane = lax.broadcasted_iota(jnp.int32, (QB, LANES), 1)
    first = lane < SB_HEAD_DIM
    q = q_ref[...] * (SB_HEAD_DIM ** -0.5)
    qq = jnp.concatenate([jnp.where(first, q, 0.0), jnp.where(first, 0.0, q)], axis=0).astype(BF16)
    rowi = lax.broadcasted_iota(jnp.int32, (2 * QB, QB), 0)
    coli = lax.broadcasted_iota(jnp.int32, (2 * QB, QB), 1)
    r1 = lax.broadcasted_iota(jnp.int32, (QB, QB), 0)
    c1 = lax.broadcasted_iota(jnp.int32, (QB, QB), 1)
    later = jnp.where(r1 > c1, 1.0, 0.0).astype(BF16)
    strict = coli < jnp.bitwise_and(rowi, QB - 1)
    b0 = bias_ref[2 * hp]
    b1 = bias_ref[2 * hp + 1]

    def block(kb, c, masked):
        start = pl.multiple_of(kb * QB, QB)
        kk = kt_ref[0, :, pl.ds(start, QB)].astype(BF16)
        vv = v_ref[pl.ds(start, QB), :].astype(BF16)
        z = _dot(qq, kk)
        z = jnp.concatenate([z[:QB] + b0, z[QB:] + b1], axis=0)
        nlf = _softplus(z)
        if masked:
            nlf = jnp.where(strict, nlf, 0.0)
        after = _dot(nlf.astype(BF16), later) + jnp.concatenate([c] * (QB // LANES), axis=1)
        w = jnp.exp((z - nlf) - after)
        if masked:
            w = jnp.where(strict, w, 0.0)
        return c + jnp.sum(nlf, axis=1, keepdims=True), _dot(w.astype(BF16), vv)

    c, pv = block(i, jnp.zeros((2 * QB, LANES), F32), True)
    c_scr[...] = c
    acc_scr[...] = pv

    @pl.loop(0, i // 2)
    def _(t):
        kb = i - 1 - 2 * t
        c1, pv1 = block(kb, c_scr[...], False)
        c2, pv2 = block(kb - 1, c1, False)
        c_scr[...] = c2
        acc_scr[...] += pv1 + pv2

    @pl.when(i % 2 == 1)
    def _():
        _, pv1 = block(0, c_scr[...], False)
        acc_scr[...] += pv1

    o_ref[...] = jnp.where(first, acc_scr[:QB], acc_scr[QB:])


def _sb_prompt(q, kt, v, bias, batch, seq):
    QB = 2 * SB_QBLOCK
    nq = seq // QB
    npair = SB_W // LANES
    qspec = pl.BlockSpec((QB, LANES), lambda b, hp, i: (b * nq + i, hp))
    ktspec = pl.BlockSpec((1, LANES, seq), lambda b, hp, i: (b, hp, 0))
    vspec = pl.BlockSpec((seq, LANES), lambda b, hp, i: (b, hp))
    return pl.pallas_call(
        _sb_prompt_kernel,
        grid=(batch, npair, nq),
        in_specs=[pl.BlockSpec(memory_space=pltpu.SMEM), qspec, ktspec, vspec],
        out_specs=qspec,
        out_shape=jax.ShapeDtypeStruct((batch * seq, SB_W), F32),
        scratch_shapes=[pltpu.VMEM((2 * QB, LANES), F32), pltpu.VMEM((2 * QB, LANES), F32)],
        compiler_params=_cparams(("parallel", "parallel", "arbitrary"), 32),
        name="sb_prompt",
    )(bias, q, kt, v)


def _sb_decode_kernel(pt_ref, bias_ref, q_ref, *rest):
    NP = PAGES_PER_STEP
    k_refs = rest[:NP]
    v_refs = rest[NP:2 * NP]
    o_ref = rest[2 * NP]
    acc_scr, carry_scr = rest[2 * NP + 1:]
    s = pl.program_id(1)

    @pl.when(s == 0)
    def _():
        acc_scr[...] = jnp.zeros_like(acc_scr)
        carry_scr[...] = jnp.zeros_like(carry_scr)

    H = SB_HEADS
    D = SB_HEAD_DIM
    P = k_refs[0].shape[4]
    sub = lax.broadcasted_iota(jnp.int32, (H, P), 0)
    bias = jnp.zeros((H, P), F32)
    for h in range(H):
        bias = jnp.where(sub == h, bias_ref[h], bias)
    rowi = lax.broadcasted_iota(jnp.int32, (P, P), 0)
    coli = lax.broadcasted_iota(jnp.int32, (P, P), 1)
    later = jnp.where(rowi > coli, 1.0, 0.0).astype(BF16)

    zs = []
    for p in range(NP):
        zp = bias
        for h in range(H):
            zh = jnp.sum(k_refs[p][0, 0, h] * q_ref[0, h], axis=0, keepdims=True)
            zp = jnp.where(sub == h, zh * (D ** -0.5) + bias, zp)
        zs.append(zp)
    z = jnp.concatenate(zs, axis=0)
    nlf = _softplus(z)
    after = _dot(nlf.astype(BF16), later)
    run = carry_scr[...]
    ws = [None] * NP
    for p in reversed(range(NP)):
        rs = slice(p * H, (p + 1) * H)
        ws[p] = jnp.exp((z[rs] - nlf[rs]) - (after[rs] + run))
        run = run + jnp.sum(nlf[rs], axis=1, keepdims=True)
    carry_scr[...] = run
    for h in range(H):
        a = acc_scr[h]
        for p in range(NP):
            a = a + v_refs[p][0, 0, h] * ws[p][h:h + 1, :]
        acc_scr[h] = a

    @pl.when(s == pl.num_programs(1) - 1)
    def _():
        ones = jnp.ones((H, P), BF16)
        subd = lax.broadcasted_iota(jnp.int32, (H, D), 0)
        out = jnp.zeros((H, D), F32)
        for h in range(H):
            hi, lo = _split_bf16(acc_scr[h])
            out = jnp.where(subd == h, _dot_nt(ones, hi) + _dot_nt(ones, lo), out)
        o_ref[0] = out


def _sb_decode(q, cache_k, cache_v, layer, page_table, bias):
    nseq = q.shape[0]
    n_pages = page_table.shape[1]
    P = cache_k.shape[2]
    NP = PAGES_PER_STEP
    nsteps = n_pages // NP
    kt = jnp.transpose(cache_k, (0, 1, 3, 4, 2))
    vt = jnp.transpose(cache_v, (0, 1, 3, 4, 2))
    qb = jnp.broadcast_to(q.reshape(nseq, SB_HEADS, SB_HEAD_DIM, 1),
                          (nseq, SB_HEADS, SB_HEAD_DIM, P))

    def page_spec(p):
        return pl.BlockSpec((1, 1, SB_HEADS, SB_HEAD_DIM, P),
                            lambda b, s, pt: (layer, pt[b, (nsteps - 1 - s) * NP + p], 0, 0, 0))

    grid_spec = pltpu.PrefetchScalarGridSpec(
        num_scalar_prefetch=1,
        grid=(nseq, nsteps),
        in_specs=[pl.BlockSpec(memory_space=pltpu.SMEM),
                  pl.BlockSpec((1, SB_HEADS, SB_HEAD_DIM, P), lambda b, s, pt: (b, 0, 0, 0))]
        + [page_spec(p) for p in range(NP)] * 2,
        out_specs=pl.BlockSpec((1, SB_HEADS, SB_HEAD_DIM), lambda b, s, pt: (b, 0, 0)),
        scratch_shapes=[pltpu.VMEM((SB_HEADS, SB_HEAD_DIM, P), F32),
                        pltpu.VMEM((SB_HEADS, P), F32)],
    )
    out = pl.pallas_call(
        _sb_decode_kernel,
        grid_spec=grid_spec,
        out_shape=jax.ShapeDtypeStruct((nseq, SB_HEADS, SB_HEAD_DIM), F32),
        compiler_params=_cparams(("parallel", "arbitrary"), 32),
        name="sb_decode",
    )(page_table, bias, qb, *([kt] * NP), *([vt] * NP))
    return out.reshape(nseq, SB_W)


def _mem_kv_kernel(m_ref, g_ref, w_ref, kn_ref, mk_ref, mv_ref):
    h = _rms(m_ref[...], g_ref[...]).astype(BF16)
    kv = _dot(h, w_ref[...])
    for hd in range(MEM_HEADS):
        hs = slice(hd * MEM_HEAD_DIM, (hd + 1) * MEM_HEAD_DIM)
        mk_ref[:, hs] = _rms(kv[:, hs], kn_ref[...])
    mv_ref[...] = kv[:, MEM_WIDTH:]


def _mem_kv(mem, gain, w, k_norm):
    T, D = mem.shape
    tm = 256
    const = lambda i: (0, 0)
    return pl.pallas_call(
        _mem_kv_kernel,
        grid=(T // tm,),
        in_specs=[pl.BlockSpec((tm, D), lambda i: (i, 0)), pl.BlockSpec((1, D), const),
                  pl.BlockSpec(w.shape, const), pl.BlockSpec((1, MEM_HEAD_DIM), const)],
        out_specs=[pl.BlockSpec((tm, MEM_WIDTH), lambda i: (i, 0))] * 2,
        out_shape=[jax.ShapeDtypeStruct((T, MEM_WIDTH), F32)] * 2,
        compiler_params=_cparams(("parallel",), 32),
        name="mem_kv",
    )(mem, gain.reshape(1, D), w, k_norm.reshape(1, MEM_HEAD_DIM))


def _mem_attn_kernel(q_ref, mk_ref, mv_ref, qn_ref, o_ref):
    tm = q_ref.shape[1]
    q = q_ref[0]
    if tm < 8:
        q = jnp.broadcast_to(q, (8, MEM_WIDTH))
    for hd in range(MEM_HEADS):
        hs = slice(hd * MEM_HEAD_DIM, (hd + 1) * MEM_HEAD_DIM)
        qh = _rms(q[:, hs], qn_ref[...]).astype(BF16)
        s = _dot_nt(qh, mk_ref[0, :, hs].astype(BF16)) * (MEM_HEAD_DIM ** -0.5)
        e = jnp.exp(s - jnp.max(s, axis=-1, keepdims=True))
        p = e / jnp.sum(e, axis=-1, keepdims=True)
        o = _dot(p.astype(BF16), mv_ref[0, :, hs].astype(BF16))
        o_ref[0, :, hs] = o[:tm]


def _mem_attn(mq, mk, mv, q_norm):
    nb, L, _ = mq.shape
    n_mem = mk.shape[1]
    tm = min(L, 256)
    qspec = pl.BlockSpec((1, tm, MEM_WIDTH), lambda b, i: (b, i, 0))
    kvspec = pl.BlockSpec((1, n_mem, MEM_WIDTH), lambda b, i: (b, 0, 0))
    return pl.pallas_call(
        _mem_attn_kernel,
        grid=(nb, L // tm),
        in_specs=[qspec, kvspec, kvspec, pl.BlockSpec((1, MEM_HEAD_DIM), lambda b, i: (0, 0))],
        out_specs=qspec,
        out_shape=jax.ShapeDtypeStruct(mq.shape, F32),
        compiler_params=_cparams(("parallel", "arbitrary"), 32),
        name="mem_attn",
    )(mq, mk, mv, q_norm.reshape(1, MEM_HEAD_DIM))


def _out_proj_kernel(x_ref, tok_ref, mo_ref, w_ref, g_ref, x1_ref, h_ref):
    tw = tok_ref.shape[1]
    y = _dot(tok_ref[...].astype(BF16), w_ref[:tw, :]) + _dot(mo_ref[...].astype(BF16), w_ref[tw:, :])
    x1 = x_ref[...] + y
    x1_ref[...] = x1
    h_ref[...] = _rms(x1, g_ref[...]).astype(BF16)


def _out_proj(x, tok, mo, w, gain):
    T, D = x.shape
    tm = min(T, 256)
    const = lambda i: (0, 0)
    rows = lambda i: (i, 0)
    return pl.pallas_call(
        _out_proj_kernel,
        grid=(T // tm,),
        in_specs=[pl.BlockSpec((tm, D), rows), pl.BlockSpec((tm, tok.shape[1]), rows),
                  pl.BlockSpec((tm, mo.shape[1]), rows), pl.BlockSpec(w.shape, const),
                  pl.BlockSpec((1, D), const)],
        out_specs=[pl.BlockSpec((tm, D), rows)] * 2,
        out_shape=[jax.ShapeDtypeStruct((T, D), F32), jax.ShapeDtypeStruct((T, D), BF16)],
        compiler_params=_cparams(("parallel",), 32),
        name="out_proj",
    )(x, tok, mo, w, gain.reshape(1, D))


def _ffn_prompt_kernel(h_ref, x_ref, wu_ref, cw_ref, cb_ref, wd_ref, o_ref, cs_ref, carry_scr,
                       act_scr):
    i = pl.program_id(1)

    @pl.when(i == 0)
    def _():
        carry_scr[...] = jnp.zeros_like(carry_scr)

    tm = h_ref.shape[0]
    FC = FFN_CHUNK
    h = h_ref[...]
    rowi = lax.broadcasted_iota(jnp.int32, (tm, FC), 0)
    for c in range(D_FF // FC):
        cs = slice(c * FC, (c + 1) * FC)
        gate = _dot(h, wu_ref[:, cs])
        val = _dot(h, wu_ref[:, D_FF + c * FC:D_FF + (c + 1) * FC])
        p0 = carry_scr[0:1, cs]
        p1 = carry_scr[1:2, cs]
        g1 = jnp.where(rowi == 0, p1, pltpu.roll(gate, 1, 0))
        g2 = jnp.where(rowi == 0, p0, jnp.where(rowi == 1, p1, pltpu.roll(gate, 2, 0)))
        a = cb_ref[:, cs] + g2 * cw_ref[0:1, cs] + g1 * cw_ref[1:2, cs] + gate * cw_ref[2:3, cs]
        act_scr[:, cs] = (_silu(a) * val).astype(BF16)
        last = gate[tm - (FFN_CONV - 1):, :]
        carry_scr[0:FFN_CONV - 1, cs] = last
        cs_ref[0, :, cs] = last
    o_ref[...] = x_ref[...] + _dot(act_scr[...], wd_ref[...])


def _ffn_prompt(h, x1, w_up, conv_w, conv_b, w_down, batch, seq):
    T, D = x1.shape
    tm = 512
    nb = seq // tm
    rows = lambda b, i: (b * nb + i, 0)
    const = lambda b, i: (0, 0)
    once = pl.Buffered(1)
    return pl.pallas_call(
        _ffn_prompt_kernel,
        grid=(batch, nb),
        in_specs=[pl.BlockSpec((tm, D), rows), pl.BlockSpec((tm, D), rows),
                  pl.BlockSpec(w_up.shape, const, pipeline_mode=once),
                  pl.BlockSpec(conv_w.shape, const), pl.BlockSpec((1, D_FF), const),
                  pl.BlockSpec(w_down.shape, const, pipeline_mode=once)],
        out_specs=[pl.BlockSpec((tm, D), rows),
                   pl.BlockSpec((1, FFN_CONV - 1, D_FF), lambda b, i: (b, 0, 0))],
        out_shape=[jax.ShapeDtypeStruct((T, D), F32),
                   jax.ShapeDtypeStruct((batch, FFN_CONV - 1, D_FF), F32)],
        scratch_shapes=[pltpu.VMEM((8, D_FF), F32), pltpu.VMEM((tm, D_FF), BF16)],
        compiler_params=_cparams(("parallel", "arbitrary"), 48),
        name="ffn_prompt",
    )(h, x1, w_up, conv_w, conv_b.reshape(1, D_FF), w_down)


def _ffn_step_kernel(h_ref, x_ref, wg_ref, wv_ref, cw_ref, cb_ref, wd_ref, p0_ref, p1_ref,
                     o_ref, gate_ref):
    @pl.when(pl.program_id(0) == 0)
    def _():
        o_ref[...] = x_ref[...]

    h = h_ref[...]
    gate = _dot(h, wg_ref[...])
    val = _dot(h, wv_ref[...])
    a = (cb_ref[...] + p0_ref[...] * cw_ref[0:1, :] + p1_ref[...] * cw_ref[1:2, :]
         + gate * cw_ref[2:3, :])
    o_ref[...] += _dot((_silu(a) * val).astype(BF16), wd_ref[...])
    gate_ref[...] = gate


def _ffn_step(h, x1, w_up, conv_w, conv_b, w_down, prev):
    T, D = x1.shape
    FC = FFN_CHUNK
    nch = D_FF // FC
    const = lambda c: (0, 0)
    chunk = lambda c: (0, c)
    out, gate = pl.pallas_call(
        _ffn_step_kernel,
        grid=(nch,),
        in_specs=[pl.BlockSpec((T, D), const), pl.BlockSpec((T, D), const),
                  pl.BlockSpec((D, FC), chunk), pl.BlockSpec((D, FC), lambda c: (0, nch + c)),
                  pl.BlockSpec((FFN_CONV, FC), chunk), pl.BlockSpec((1, FC), chunk),
                  pl.BlockSpec((FC, D), lambda c: (c, 0)),
                  pl.BlockSpec((T, FC), chunk), pl.BlockSpec((T, FC), chunk)],
        out_specs=[pl.BlockSpec((T, D), const), pl.BlockSpec((T, FC), chunk)],
        out_shape=[jax.ShapeDtypeStruct((T, D), F32), jax.ShapeDtypeStruct((T, D_FF), F32)],
        compiler_params=_cparams(("arbitrary",), 32),
        name="ffn_step",
    )(h, x1, w_up, w_up, conv_w, conv_b.reshape(1, D_FF), w_down, prev[:, 0], prev[:, 1])
    return out, jnp.stack([prev[:, 1], gate], axis=1)


def kernel(x_prompt, x_sample, state_gla, cache_sb_k, cache_sb_v, cache_mem_k, cache_mem_v, state_ffn_conv, page_table, mem_prompt, norm_mix, w_in_gla, w_gate_up, b_gate, gla_out_norm, w_in_sb, sb_bias, mem_norm, w_mem_kv, mem_q_norm, mem_k_norm, w_out, norm_ffn, w_ffn_up, ffn_conv_w, ffn_conv_b, w_ffn_down):
    batch, seq, d_model = x_prompt.shape
    nseq = x_sample.shape[0]
    n_mem = mem_prompt.shape[1]
    depth = norm_mix.shape[0]
    gd0 = 2 * GLA_QK + 2 * GLA_VW
    gla_widths = (GLA_QK, GLA_QK, GLA_VW, GLA_VW, MEM_WIDTH)
    sb_widths = (SB_W, SB_W, SB_W, MEM_WIDTH)

    def gla_weights(j):
        w = w_in_gla[j]
        w_main = jnp.concatenate([w[:, :gd0], w[:, gd0 + GLA_GATE_RANK:]], axis=1).astype(BF16)
        w_gd = jnp.pad(w[:, gd0:gd0 + GLA_GATE_RANK], ((0, 0), (0, LANES - GLA_GATE_RANK))).astype(BF16)
        w_gu = jnp.pad(w_gate_up[j], ((0, LANES - GLA_GATE_RANK), (0, 0))).astype(BF16)
        return w_main, (w_gd, w_gu, b_gate[j])

    per_layer_bf16 = lambda w: [w[i].astype(BF16) for i in range(w.shape[0])]
    w_out_b = per_layer_bf16(w_out)
    w_up_b = per_layer_bf16(w_ffn_up)
    w_down_b = per_layer_bf16(w_ffn_down)
    w_kv_b = per_layer_bf16(w_mem_kv)
    w_sb_b = per_layer_bf16(w_in_sb)
    gla_w = [gla_weights(j) for j in range(w_in_gla.shape[0])]

    x = x_prompt.reshape(batch * seq, d_model)
    mem = mem_prompt.reshape(batch * n_mem, d_model)
    p_gla, p_sbk, p_sbv, p_mk, p_mv, p_conv = [], [], [], [], [], []
    for i in range(depth):
        j = i // 2
        if i % 2 == 0:
            w_main, gate = gla_w[j]
            q, k, v, g, mq, la = _proj(x, norm_mix[i], w_main, gla_widths, gate)
            tok, s_fin = _gla_prompt(q, k, la, v, g, gla_out_norm[j], batch, seq)
            p_gla.append(s_fin)
        else:
            q, kt, v, vt, mq = _proj_sb_prompt(x, norm_mix[i], w_sb_b[j], batch, seq)
            tok = _sb_prompt(q, kt, v, sb_bias[j], batch, seq)
            by_pos = lambda t: t.reshape(batch, SB_HEADS, SB_HEAD_DIM, seq).transpose(0, 3, 1, 2)
            p_sbk.append(by_pos(kt))
            p_sbv.append(by_pos(vt))
        mk, mv = _mem_kv(mem, mem_norm[i], w_kv_b[i], mem_k_norm[i])
        p_mk.append(mk.reshape(batch, n_mem, MEM_HEADS, MEM_HEAD_DIM))
        p_mv.append(mv.reshape(batch, n_mem, MEM_HEADS, MEM_HEAD_DIM))
        mo = _mem_attn(mq.reshape(batch, seq, MEM_WIDTH), mk.reshape(batch, n_mem, MEM_WIDTH),
                       mv.reshape(batch, n_mem, MEM_WIDTH), mem_q_norm[i])
        x1, h2 = _out_proj(x, tok, mo.reshape(batch * seq, MEM_WIDTH), w_out_b[i], norm_ffn[i])
        x, conv = _ffn_prompt(h2, x1, w_up_b[i], ffn_conv_w[i], ffn_conv_b[i], w_down_b[i], batch, seq)
        p_conv.append(conv)
    y_prompt = x.reshape(batch, seq, d_model)

    dec_seq = x_sample.shape[1]
    x = x_sample.reshape(nseq * dec_seq, d_model)
    s_gla, s_sbk, s_sbv, s_conv = [], [], [], []
    for i in range(depth):
        j = i // 2
        if i % 2 == 0:
            w_main, gate = gla_w[j]
            q, k, v, g, mq, la = _proj(x, norm_mix[i], w_main, gla_widths, gate)
            tok, s_new = _gla_step(q, k, la, v, g, state_gla[j], gla_out_norm[j])
            s_gla.append(s_new)
        else:
            q, k, v, mq = _proj(x, norm_mix[i], w_sb_b[j], sb_widths)
            tok = _sb_decode(q, cache_sb_k, cache_sb_v, j, page_table, sb_bias[j])
            s_sbk.append(k.reshape(nseq, dec_seq, SB_HEADS, SB_HEAD_DIM))
            s_sbv.append(v.reshape(nseq, dec_seq, SB_HEADS, SB_HEAD_DIM))
        mo = _mem_attn(mq.reshape(nseq, dec_seq, MEM_WIDTH),
                       cache_mem_k[i].reshape(nseq, n_mem, MEM_WIDTH),
                       cache_mem_v[i].reshape(nseq, n_mem, MEM_WIDTH), mem_q_norm[i])
        x1, h2 = _out_proj(x, tok, mo.reshape(nseq * dec_seq, MEM_WIDTH), w_out_b[i], norm_ffn[i])
        x, conv = _ffn_step(h2, x1, w_up_b[i], ffn_conv_w[i], ffn_conv_b[i], w_down_b[i],
                            state_ffn_conv[i])
        s_conv.append(conv)
    y_sample = x.reshape(nseq, dec_seq, d_model)

    return (y_prompt, y_sample,
            jnp.stack(p_gla), jnp.stack(s_gla),
            jnp.stack(p_sbk), jnp.stack(p_sbv), jnp.stack(s_sbk), jnp.stack(s_sbv),
            jnp.stack(p_mk), jnp.stack(p_mv),
            jnp.stack(p_conv), jnp.stack(s_conv))
```

```python
import functools

import jax
import jax.numpy as jnp
from jax import lax
from jax.experimental import pallas as pl
from jax.experimental.pallas import tpu as pltpu

F32 = jnp.float32
BF16 = jnp.bfloat16

GLA_HEADS = 4
GLA_DK = 64
GLA_DV = 128
GLA_GATE_RANK = 16
GLA_TAU = 16.0
GLA_CHUNK = 64
SB_HEADS = 8
SB_HEAD_DIM = 64
SB_QBLOCK = 128
MEM_HEADS = 4
MEM_HEAD_DIM = 128
D_FF = 2816
FFN_CONV = 3
EPS = 1e-6

GLA_QK = GLA_HEADS * GLA_DK
GLA_VW = GLA_HEADS * GLA_DV
SB_W = SB_HEADS * SB_HEAD_DIM
MEM_WIDTH = MEM_HEADS * MEM_HEAD_DIM

LOG2E = 1.4426950408889634
LANES = 128
FFN_CHUNK = 256
PAGES_PER_STEP = 16


def _cparams(semantics, vmem_mib):
    return pltpu.CompilerParams(dimension_semantics=semantics,
                                vmem_limit_bytes=vmem_mib << 20)


def _dot(a, b):
    return jnp.dot(a, b, preferred_element_type=F32)


def _dot_nt(a, b):
    return lax.dot_general(a, b, (((1,), (1,)), ((), ())), preferred_element_type=F32)


def _rms(xf, g):
    ms = jnp.mean(xf * xf, axis=-1, keepdims=True)
    return xf * lax.rsqrt(ms + EPS) * g


def _log_sigmoid_pair(z):
    t = jnp.log1p(jnp.exp(-jnp.abs(z)))
    return jnp.minimum(z, 0.0) - t, jnp.minimum(-z, 0.0) - t


def _softplus(z):
    return jnp.maximum(z, 0.0) + jnp.log(1.0 + jnp.exp2(jnp.abs(z) * (-LOG2E)))


def _silu(x):
    return x * jax.nn.sigmoid(x)


def _split_bf16(x):
    hi = x.astype(BF16)
    lo = (x - hi.astype(F32)).astype(BF16)
    return hi, lo


def _dot_split(x, m):
    hi, lo = _split_bf16(x)
    return _dot(hi, m) + _dot(lo, m)


def _dot_split_left(m, x):
    hi, lo = _split_bf16(x)
    return _dot(m, hi) + _dot(m, lo)


def _proj_kernel(*refs, widths, gla):
    if gla:
        x_ref, g_ref, w_ref, wgd_ref, wgu_ref, bg_ref = refs[:6]
        outs = refs[6:]
    else:
        x_ref, g_ref, w_ref = refs[:3]
        outs = refs[3:]
    h = _rms(x_ref[...], g_ref[...]).astype(BF16)
    z = _dot(h, w_ref[...])
    off = 0
    for o_ref, wd in zip(outs, widths):
        o_ref[...] = z[:, off:off + wd]
        off += wd
    if gla:
        gd = _dot(h, wgd_ref[...])
        pre = _dot(gd.astype(BF16), wgu_ref[...]) + bg_ref[...]
        ls, _ = _log_sigmoid_pair(pre)
        outs[-1][...] = ls * (1.0 / GLA_TAU)


def _proj(x, gain, w, widths, gate=None):
    T, D = x.shape
    tm = min(T, 512)
    const = lambda i: (0, 0)
    in_specs = [pl.BlockSpec((tm, D), lambda i: (i, 0)),
                pl.BlockSpec((1, D), const),
                pl.BlockSpec(w.shape, const)]
    args = [x, gain.reshape(1, D), w]
    out_widths = list(widths)
    if gate is not None:
        wgd, wgu, bg = gate
        in_specs += [pl.BlockSpec(wgd.shape, const), pl.BlockSpec(wgu.shape, const),
                     pl.BlockSpec((1, GLA_QK), const)]
        args += [wgd, wgu, bg.reshape(1, GLA_QK)]
        out_widths.append(GLA_QK)
    return pl.pallas_call(
        functools.partial(_proj_kernel, widths=tuple(widths), gla=gate is not None),
        grid=(T // tm,),
        in_specs=in_specs,
        out_specs=[pl.BlockSpec((tm, wd), lambda i: (i, 0)) for wd in out_widths],
        out_shape=[jax.ShapeDtypeStruct((T, wd), F32) for wd in out_widths],
        compiler_params=_cparams(("parallel",), 40),
        name="proj_gla" if gate is not None else "proj_sb",
    )(*args)


def _proj_sb_prompt_kernel(x_ref, g_ref, w_ref, q_ref, kt_ref, v_ref, vt_ref, mq_ref):
    h = _rms(x_ref[...], g_ref[...]).astype(BF16)
    z = _dot(h, w_ref[...])
    q_ref[...] = z[:, :SB_W]
    k = z[:, SB_W:2 * SB_W]
    v = z[:, 2 * SB_W:3 * SB_W]
    kt_ref[0] = k.T
    v_ref[...] = v
    vt_ref[0] = v.T
    mq_ref[...] = z[:, 3 * SB_W:]


def _proj_sb_prompt(x, gain, w, batch, seq):
    T, D = x.shape
    tm = 512
    nb = seq // tm
    const = lambda i: (0, 0)
    rows = lambda i: (i, 0)
    cols = lambda i: (i // nb, 0, i % nb)
    row_spec = pl.BlockSpec((tm, SB_W), rows)
    col_spec = pl.BlockSpec((1, SB_W, tm), cols)
    row_shape = jax.ShapeDtypeStruct((T, SB_W), F32)
    col_shape = jax.ShapeDtypeStruct((batch, SB_W, seq), F32)
    return pl.pallas_call(
        _proj_sb_prompt_kernel,
        grid=(T // tm,),
        in_specs=[pl.BlockSpec((tm, D), rows), pl.BlockSpec((1, D), const),
                  pl.BlockSpec(w.shape, const)],
        out_specs=[row_spec, col_spec, row_spec, col_spec, row_spec],
        out_shape=[row_shape, col_shape, row_shape, col_shape, row_shape],
        compiler_params=_cparams(("parallel",), 48),
        name="proj_sb_prompt",
    )(x, gain.reshape(1, D), w)


def _gla_prompt_kernel(q_ref, k_ref, la_ref, v_ref, g_ref, on_ref, tok_ref, s_ref, s_scr):
    i = pl.program_id(1)

    @pl.when(i == 0)
    def _():
        s_scr[...] = jnp.zeros_like(s_scr)

    R = q_ref.shape[0]
    C = GLA_CHUNK
    row = lax.broadcasted_iota(jnp.int32, (R, R), 0)
    col = lax.broadcasted_iota(jnp.int32, (R, R), 1)
    same = (row // C) == (col // C)
    causal = jnp.logical_and(same, col <= row)
    ltri = jnp.where(causal, 1.0, 0.0).astype(BF16)
    ones_bd = jnp.where(same, 1.0, 0.0).astype(BF16)

    la = la_ref[...]
    b = _dot_split_left(ltri, la)
    bt = _dot_split_left(ones_bd, la)
    q_dec = q_ref[...] * (GLA_DK ** -0.5) * jnp.exp(b)
    k = k_ref[...]
    k_dec = k * jnp.exp(-b)
    kte_t = (k * jnp.exp(bt - b)).T
    dec_t = jnp.exp(bt).T

    for h in range(GLA_HEADS):
        ks = slice(h * GLA_DK, (h + 1) * GLA_DK)
        vs = slice(h * GLA_DV, (h + 1) * GLA_DV)
        qh = q_dec[:, ks].astype(BF16)
        kh = k_dec[:, ks].astype(BF16)
        att = jnp.where(causal, _dot_nt(qh, kh), 0.0)
        vh = v_ref[:, vs].astype(BF16)
        o = _dot(att.astype(BF16), vh)
        s = s_scr[h]
        inter = []
        for c in range(R // C):
            rs = slice(c * C, (c + 1) * C)
            inter.append(_dot(qh[rs], s.astype(BF16)))
            kv = _dot(kte_t[ks, rs].astype(BF16), vh[rs])
            s = dec_t[ks, c * C:c * C + 1] * s + kv
        s_scr[h] = s
        o = o + jnp.concatenate(inter, axis=0)
        o = _rms(o, on_ref[...])
        tok_ref[:, vs] = o * _silu(g_ref[:, vs])

    @pl.when(i == pl.num_programs(1) - 1)
    def _():
        s_ref[0] = s_scr[...]


def _gla_prompt(q, k, la, v, g, out_norm, batch, seq):
    R = 256
    nb = seq // R
    rows = lambda b, i: (b * nb + i, 0)
    return pl.pallas_call(
        _gla_prompt_kernel,
        grid=(batch, nb),
        in_specs=[pl.BlockSpec((R, GLA_QK), rows), pl.BlockSpec((R, GLA_QK), rows),
                  pl.BlockSpec((R, GLA_QK), rows), pl.BlockSpec((R, GLA_VW), rows),
                  pl.BlockSpec((R, GLA_VW), rows),
                  pl.BlockSpec((1, GLA_DV), lambda b, i: (0, 0))],
        out_specs=[pl.BlockSpec((R, GLA_VW), rows),
                   pl.BlockSpec((1, GLA_HEADS, GLA_DK, GLA_DV), lambda b, i: (b, 0, 0, 0))],
        out_shape=[jax.ShapeDtypeStruct((batch * seq, GLA_VW), F32),
                   jax.ShapeDtypeStruct((batch, GLA_HEADS, GLA_DK, GLA_DV), F32)],
        scratch_shapes=[pltpu.VMEM((GLA_HEADS, GLA_DK, GLA_DV), F32)],
        compiler_params=_cparams(("parallel", "arbitrary"), 32),
        name="gla_prompt",
    )(q, k, la, v, g, out_norm.reshape(1, GLA_DV))


def _gla_step_kernel(qt_ref, kt_ref, lat_ref, v_ref, g_ref, s0_ref, on_ref, tok_ref, sn_ref):
    nb = v_ref.shape[0]
    for i in range(nb):
        for h in range(GLA_HEADS):
            vs = slice(h * GLA_DV, (h + 1) * GLA_DV)
            qc = qt_ref[i, :, h:h + 1] * (GLA_DK ** -0.5)
            kc = kt_ref[i, :, h:h + 1]
            ac = jnp.exp(lat_ref[i, :, h:h + 1])
            s = ac * s0_ref[i, h] + kc * v_ref[i:i + 1, vs]
            sn_ref[i, h] = s
            o = jnp.sum(qc * s, axis=0, keepdims=True)
            o = _rms(o, on_ref[...])
            tok_ref[i:i + 1, vs] = o * _silu(g_ref[i:i + 1, vs])


def _gla_step(q, k, la, v, g, s0, out_norm):
    nseq = q.shape[0]
    nb = 8
    cols = lambda a: a.reshape(nseq, GLA_HEADS, GLA_DK).transpose(0, 2, 1)
    col_spec = pl.BlockSpec((nb, GLA_DK, GLA_HEADS), lambda i: (i, 0, 0))
    row_spec = pl.BlockSpec((nb, GLA_VW), lambda i: (i, 0))
    st_spec = pl.BlockSpec((nb, GLA_HEADS, GLA_DK, GLA_DV), lambda i: (i, 0, 0, 0))
    return pl.pallas_call(
        _gla_step_kernel,
        grid=(nseq // nb,),
        in_specs=[col_spec, col_spec, col_spec, row_spec, row_spec, st_spec,
                  pl.BlockSpec((1, GLA_DV), lambda i: (0, 0))],
        out_specs=[row_spec, st_spec],
        out_shape=[jax.ShapeDtypeStruct((nseq, GLA_VW), F32),
                   jax.ShapeDtypeStruct(s0.shape, F32)],
        compiler_params=_cparams(("parallel",), 32),
        name="gla_step",
    )(cols(q), cols(k), cols(la), v, g, s0, out_norm.reshape(1, GLA_DV))


def _sb_prompt_kernel(bias_ref, q_ref, kt_ref, v_ref, o_ref, c_scr, acc_scr):
    hp = pl.program_id(1)
    i = pl.program_id(2)
    QB = q_ref.shape[0]
    lane = lax.broadcasted_iota(jnp.int32, (QB, LANES), 1)
    first = lane < SB_HEAD_DIM
    q = q_ref[...] * (SB_HEAD_DIM ** -0.5)
    qq = jnp.concatenate([jnp.where(first, q, 0.0), jnp.where(first, 0.0, q)], axis=0).astype(BF16)
    rowi = lax.broadcasted_iota(jnp.int32, (2 * QB, QB), 0)
    coli = lax.broadcasted_iota(jnp.int32, (2 * QB, QB), 1)
    r1 = lax.broadcasted_iota(jnp.int32, (QB, QB), 0)
    c1 = lax.broadcasted_iota(jnp.int32, (QB, QB), 1)
    later = jnp.where(r1 > c1, 1.0, 0.0).astype(BF16)
    strict = coli < jnp.bitwise_and(rowi, QB - 1)
    b0 = bias_ref[2 * hp]
    b1 = bias_ref[2 * hp + 1]

    def block(kb, c, masked):
        start = pl.multiple_of(kb * QB, QB)
        kk = kt_ref[0, :, pl.ds(start, QB)].astype(BF16)
        vv = v_ref[pl.ds(start, QB), :].astype(BF16)
        z = _dot(qq, kk)
        z = jnp.concatenate([z[:QB] + b0, z[QB:] + b1], axis=0)
        nlf = _softplus(z)
        if masked:
            nlf = jnp.where(strict, nlf, 0.0)
        after = _dot(nlf.astype(BF16), later) + jnp.concatenate([c] * (QB // LANES), axis=1)
        w = jnp.exp((z - nlf) - after)
        if masked:
            w = jnp.where(strict, w, 0.0)
        return c + jnp.sum(nlf, axis=1, keepdims=True), _dot(w.astype(BF16), vv)

    c, pv = block(i, jnp.zeros((2 * QB, LANES), F32), True)
    c_scr[...] = c
    acc_scr[...] = pv

    @pl.loop(0, i // 2)
    def _(t):
        kb = i - 1 - 2 * t
        c1, pv1 = block(kb, c_scr[...], False)
        c2, pv2 = block(kb - 1, c1, False)
        c_scr[...] = c2
        acc_scr[...] += pv1 + pv2

    @pl.when(i % 2 == 1)
    def _():
        _, pv1 = block(0, c_scr[...], False)
        acc_scr[...] += pv1

    o_ref[...] = jnp.where(first, acc_scr[:QB], acc_scr[QB:])


def _sb_prompt(q, kt, v, bias, batch, seq):
    QB = 2 * SB_QBLOCK
    nq = seq // QB
    npair = SB_W // LANES
    qspec = pl.BlockSpec((QB, LANES), lambda b, hp, i: (b * nq + i, hp))
    ktspec = pl.BlockSpec((1, LANES, seq), lambda b, hp, i: (b, hp, 0))
    vspec = pl.BlockSpec((seq, LANES), lambda b, hp, i: (b, hp))
    return pl.pallas_call(
        _sb_prompt_kernel,
        grid=(batch, npair, nq),
        in_specs=[pl.BlockSpec(memory_space=pltpu.SMEM), qspec, ktspec, vspec],
        out_specs=qspec,
        out_shape=jax.ShapeDtypeStruct((batch * seq, SB_W), F32),
        scratch_shapes=[pltpu.VMEM((2 * QB, LANES), F32), pltpu.VMEM((2 * QB, LANES), F32)],
        compiler_params=_cparams(("parallel", "parallel", "arbitrary"), 32),
        name="sb_prompt",
    )(bias, q, kt, v)


def _sb_decode_kernel(pt_ref, bias_ref, q_ref, *rest):
    NP = PAGES_PER_STEP
    k_refs = rest[:NP]
    v_refs = rest[NP:2 * NP]
    o_ref = rest[2 * NP]
    acc_scr, carry_scr = rest[2 * NP + 1:]
    s = pl.program_id(1)

    @pl.when(s == 0)
    def _():
        acc_scr[...] = jnp.zeros_like(acc_scr)
        carry_scr[...] = jnp.zeros_like(carry_scr)

    H = SB_HEADS
    D = SB_HEAD_DIM
    P = k_refs[0].shape[4]
    sub = lax.broadcasted_iota(jnp.int32, (H, P), 0)
    bias = jnp.zeros((H, P), F32)
    for h in range(H):
        bias = jnp.where(sub == h, bias_ref[h], bias)
    rowi = lax.broadcasted_iota(jnp.int32, (P, P), 0)
    coli = lax.broadcasted_iota(jnp.int32, (P, P), 1)
    later = jnp.where(rowi > coli, 1.0, 0.0).astype(BF16)

    zs = []
    for p in range(NP):
        zp = bias
        for h in range(H):
            zh = jnp.sum(k_refs[p][0, 0, h] * q_ref[0, h], axis=0, keepdims=True)
            zp = jnp.where(sub == h, zh * (D ** -0.5) + bias, zp)
        zs.append(zp)
    z = jnp.concatenate(zs, axis=0)
    nlf = _softplus(z)
    after = _dot(nlf.astype(BF16), later)
    run = carry_scr[...]
    ws = [None] * NP
    for p in reversed(range(NP)):
        rs = slice(p * H, (p + 1) * H)
        ws[p] = jnp.exp((z[rs] - nlf[rs]) - (after[rs] + run))
        run = run + jnp.sum(nlf[rs], axis=1, keepdims=True)
    carry_scr[...] = run
    for h in range(H):
        a = acc_scr[h]
        for p in range(NP):
            a = a + v_refs[p][0, 0, h] * ws[p][h:h + 1, :]
        acc_scr[h] = a

    @pl.when(s == pl.num_programs(1) - 1)
    def _():
        ones = jnp.ones((H, P), BF16)
        subd = lax.broadcasted_iota(jnp.int32, (H, D), 0)
        out = jnp.zeros((H, D), F32)
        for h in range(H):
            hi, lo = _split_bf16(acc_scr[h])
            out = jnp.where(subd == h, _dot_nt(ones, hi) + _dot_nt(ones, lo), out)
        o_ref[0] = out


def _sb_decode(q, cache_k, cache_v, layer, page_table, bias):
    nseq = q.shape[0]
    n_pages = page_table.shape[1]
    P = cache_k.shape[2]
    NP = PAGES_PER_STEP
    nsteps = n_pages // NP
    kt = jnp.transpose(cache_k, (0, 1, 3, 4, 2))
    vt = jnp.transpose(cache_v, (0, 1, 3, 4, 2))
    qb = jnp.broadcast_to(q.reshape(nseq, SB_HEADS, SB_HEAD_DIM, 1),
                          (nseq, SB_HEADS, SB_HEAD_DIM, P))

    def page_spec(p):
        return pl.BlockSpec((1, 1, SB_HEADS, SB_HEAD_DIM, P),
                            lambda b, s, pt: (layer, pt[b, (nsteps - 1 - s) * NP + p], 0, 0, 0))

    grid_spec = pltpu.PrefetchScalarGridSpec(
        num_scalar_prefetch=1,
        grid=(nseq, nsteps),
        in_specs=[pl.BlockSpec(memory_space=pltpu.SMEM),
                  pl.BlockSpec((1, SB_HEADS, SB_HEAD_DIM, P), lambda b, s, pt: (b, 0, 0, 0))]
        + [page_spec(p) for p in range(NP)] * 2,
        out_specs=pl.BlockSpec((1, SB_HEADS, SB_HEAD_DIM), lambda b, s, pt: (b, 0, 0)),
        scratch_shapes=[pltpu.VMEM((SB_HEADS, SB_HEAD_DIM, P), F32),
                        pltpu.VMEM((SB_HEADS, P), F32)],
    )
    out = pl.pallas_call(
        _sb_decode_kernel,
        grid_spec=grid_spec,
        out_shape=jax.ShapeDtypeStruct((nseq, SB_HEADS, SB_HEAD_DIM), F32),
        compiler_params=_cparams(("parallel", "arbitrary"), 32),
        name="sb_decode",
    )(page_table, bias, qb, *([kt] * NP), *([vt] * NP))
    return out.reshape(nseq, SB_W)


def _mem_kv_kernel(m_ref, g_ref, w_ref, kn_ref, mk_ref, mv_ref, mkt_ref, mvb_ref):
    h = _rms(m_ref[...], g_ref[...]).astype(BF16)
    kv = _dot(h, w_ref[...])
    for hd in range(MEM_HEADS):
        hs = slice(hd * MEM_HEAD_DIM, (hd + 1) * MEM_HEAD_DIM)
        mk = _rms(kv[:, hs], kn_ref[...])
        mk_ref[:, hs] = mk
        mkt_ref[0, hs, :] = mk.T.astype(BF16)
    mv = kv[:, MEM_WIDTH:]
    mv_ref[...] = mv
    mvb_ref[...] = mv.astype(BF16)


def _mem_kv(mem, gain, w, k_norm, n_mem):
    T, D = mem.shape
    const = lambda i: (0, 0)
    rows = lambda i: (i, 0)
    return pl.pallas_call(
        _mem_kv_kernel,
        grid=(T // n_mem,),
        in_specs=[pl.BlockSpec((n_mem, D), rows), pl.BlockSpec((1, D), const),
                  pl.BlockSpec(w.shape, const), pl.BlockSpec((1, MEM_HEAD_DIM), const)],
        out_specs=[pl.BlockSpec((n_mem, MEM_WIDTH), rows), pl.BlockSpec((n_mem, MEM_WIDTH), rows),
                   pl.BlockSpec((1, MEM_WIDTH, n_mem), lambda i: (i, 0, 0)),
                   pl.BlockSpec((n_mem, MEM_WIDTH), rows)],
        out_shape=[jax.ShapeDtypeStruct((T, MEM_WIDTH), F32), jax.ShapeDtypeStruct((T, MEM_WIDTH), F32),
                   jax.ShapeDtypeStruct((T // n_mem, MEM_WIDTH, n_mem), BF16),
                   jax.ShapeDtypeStruct((T, MEM_WIDTH), BF16)],
        compiler_params=_cparams(("parallel",), 32),
        name="mem_kv",
    )(mem, gain.reshape(1, D), w, k_norm.reshape(1, MEM_HEAD_DIM))


def _mem_attn_step_kernel(q_ref, mk_ref, mv_ref, qn_ref, o_ref):
    for i in range(q_ref.shape[0]):
        q = _rms(q_ref[i], qn_ref[...])
        s = jnp.sum(mk_ref[i] * q, axis=-1, keepdims=True) * (MEM_HEAD_DIM ** -0.5)
        e = jnp.exp(s - jnp.max(s, axis=0, keepdims=True))
        p = e / jnp.sum(e, axis=0, keepdims=True)
        o_ref[i] = jnp.sum(p * mv_ref[i], axis=0)


def _mem_attn_step(mq, cache_k, cache_v, layer, q_norm):
    nseq = mq.shape[0]
    n_mem = cache_k.shape[2]
    nb = 8
    qspec = pl.BlockSpec((nb, MEM_HEADS, MEM_HEAD_DIM), lambda i: (i, 0, 0))
    kvspec = pl.BlockSpec((None, nb, n_mem, MEM_HEADS, MEM_HEAD_DIM), lambda i: (layer, i, 0, 0, 0))
    out = pl.pallas_call(
        _mem_attn_step_kernel,
        grid=(nseq // nb,),
        in_specs=[qspec, kvspec, kvspec, pl.BlockSpec((1, MEM_HEAD_DIM), lambda i: (0, 0))],
        out_specs=qspec,
        out_shape=jax.ShapeDtypeStruct((nseq, MEM_HEADS, MEM_HEAD_DIM), F32),
        compiler_params=_cparams(("parallel",), 32),
        name="mem_attn_step",
    )(mq.reshape(nseq, MEM_HEADS, MEM_HEAD_DIM), cache_k, cache_v, q_norm.reshape(1, MEM_HEAD_DIM))
    return out.reshape(nseq, MEM_WIDTH)


def _out_proj_kernel(x_ref, tok_ref, mo_ref, w_ref, g_ref, x1_ref, h_ref):
    tw = tok_ref.shape[1]
    y = _dot(tok_ref[...].astype(BF16), w_ref[:tw, :]) + _dot(mo_ref[...].astype(BF16), w_ref[tw:, :])
    x1 = x_ref[...] + y
    x1_ref[...] = x1
    h_ref[...] = _rms(x1, g_ref[...]).astype(BF16)


def _out_proj(x, tok, mo, w, gain):
    T, D = x.shape
    tm = min(T, 256)
    const = lambda i: (0, 0)
    rows = lambda i: (i, 0)
    return pl.pallas_call(
        _out_proj_kernel,
        grid=(T // tm,),
        in_specs=[pl.BlockSpec((tm, D), rows), pl.BlockSpec((tm, tok.shape[1]), rows),
                  pl.BlockSpec((tm, mo.shape[1]), rows), pl.BlockSpec(w.shape, const),
                  pl.BlockSpec((1, D), const)],
        out_specs=[pl.BlockSpec((tm, D), rows)] * 2,
        out_shape=[jax.ShapeDtypeStruct((T, D), F32), jax.ShapeDtypeStruct((T, D), BF16)],
        compiler_params=_cparams(("parallel",), 32),
        name="out_proj",
    )(x, tok, mo, w, gain.reshape(1, D))


def _tail_prompt_kernel(x_ref, tok_ref, mq_ref, mkt_ref, mv_ref, qn_ref, wo_ref, g_ref,
                        wu_ref, cw_ref, cb_ref, wd_ref, o_ref, cs_ref, carry_scr, h_scr, act_scr):
    i = pl.program_id(1)

    @pl.when(i == 0)
    def _():
        carry_scr[...] = jnp.zeros_like(carry_scr)

    tm = x_ref.shape[0]
    tw = tok_ref.shape[1]
    y = _dot(tok_ref[...].astype(BF16), wo_ref[:tw, :])
    for hd in range(MEM_HEADS):
        hs = slice(hd * MEM_HEAD_DIM, (hd + 1) * MEM_HEAD_DIM)
        qh = _rms(mq_ref[:, hs], qn_ref[...]).astype(BF16)
        s = _dot(qh, mkt_ref[0, hs, :]) * (MEM_HEAD_DIM ** -0.5)
        e = jnp.exp(s - jnp.max(s, axis=-1, keepdims=True))
        o = _dot(e.astype(BF16), mv_ref[:, hs]) / jnp.sum(e, axis=-1, keepdims=True)
        y = y + _dot(o.astype(BF16), wo_ref[tw + hd * MEM_HEAD_DIM:tw + (hd + 1) * MEM_HEAD_DIM, :])
    x1 = x_ref[...] + y
    o_ref[...] = x1
    h_scr[...] = _rms(x1, g_ref[...]).astype(BF16)

    FC = FFN_CHUNK
    h = h_scr[...]
    rowi = lax.broadcasted_iota(jnp.int32, (tm, FC), 0)
    for c in range(D_FF // FC):
        cs = slice(c * FC, (c + 1) * FC)
        gate = _dot(h, wu_ref[:, cs])
        val = _dot(h, wu_ref[:, D_FF + c * FC:D_FF + (c + 1) * FC])
        p0 = carry_scr[0:1, cs]
        p1 = carry_scr[1:2, cs]
        g1 = jnp.where(rowi == 0, p1, pltpu.roll(gate, 1, 0))
        g2 = jnp.where(rowi == 0, p0, jnp.where(rowi == 1, p1, pltpu.roll(gate, 2, 0)))
        a = cb_ref[:, cs] + g2 * cw_ref[0:1, cs] + g1 * cw_ref[1:2, cs] + gate * cw_ref[2:3, cs]
        act_scr[:, cs] = (_silu(a) * val).astype(BF16)
        last = gate[tm - (FFN_CONV - 1):, :]
        carry_scr[0:FFN_CONV - 1, cs] = last
        cs_ref[0, :, cs] = last
    o_ref[...] += _dot(act_scr[...], wd_ref[...])


def _tail_prompt(x, tok, mq, mkt, mvb, q_norm, w_o, gain, w_up, conv_w, conv_b, w_down, batch, seq):
    T, D = x.shape
    n_mem = mkt.shape[2]
    tm = 512
    nb = seq // tm
    rows = lambda b, i: (b * nb + i, 0)
    const = lambda b, i: (0, 0)
    once = pl.Buffered(1)
    return pl.pallas_call(
        _tail_prompt_kernel,
        grid=(batch, nb),
        in_specs=[pl.BlockSpec((tm, D), rows), pl.BlockSpec((tm, tok.shape[1]), rows),
                  pl.BlockSpec((tm, MEM_WIDTH), rows),
                  pl.BlockSpec((1, MEM_WIDTH, n_mem), lambda b, i: (b, 0, 0)),
                  pl.BlockSpec((n_mem, MEM_WIDTH), lambda b, i: (b, 0)),
                  pl.BlockSpec((1, MEM_HEAD_DIM), const),
                  pl.BlockSpec(w_o.shape, const, pipeline_mode=once),
                  pl.BlockSpec((1, D), const),
                  pl.BlockSpec(w_up.shape, const, pipeline_mode=once),
                  pl.BlockSpec(conv_w.shape, const), pl.BlockSpec((1, D_FF), const),
                  pl.BlockSpec(w_down.shape, const, pipeline_mode=once)],
        out_specs=[pl.BlockSpec((tm, D), rows),
                   pl.BlockSpec((1, FFN_CONV - 1, D_FF), lambda b, i: (b, 0, 0))],
        out_shape=[jax.ShapeDtypeStruct((T, D), F32),
                   jax.ShapeDtypeStruct((batch, FFN_CONV - 1, D_FF), F32)],
        scratch_shapes=[pltpu.VMEM((8, D_FF), F32), pltpu.VMEM((tm, D), BF16),
                        pltpu.VMEM((tm, D_FF), BF16)],
        compiler_params=_cparams(("parallel", "arbitrary"), 56),
        name="tail_prompt",
    )(x, tok, mq, mkt, mvb, q_norm.reshape(1, MEM_HEAD_DIM), w_o, gain.reshape(1, D),
      w_up, conv_w, conv_b.reshape(1, D_FF), w_down)


def _ffn_step_kernel(h_ref, x_ref, wg_ref, wv_ref, cw_ref, cb_ref, wd_ref, p0_ref, p1_ref,
                     o_ref, gate_ref):
    @pl.when(pl.program_id(0) == 0)
    def _():
        o_ref[...] = x_ref[...]

    h = h_ref[...]
    gate = _dot(h, wg_ref[...])
    val = _dot(h, wv_ref[...])
    a = (cb_ref[...] + p0_ref[...] * cw_ref[0:1, :] + p1_ref[...] * cw_ref[1:2, :]
         + gate * cw_ref[2:3, :])
    o_ref[...] += _dot((_silu(a) * val).astype(BF16), wd_ref[...])
    gate_ref[...] = gate


def _ffn_step(h, x1, w_up, conv_w, conv_b, w_down, prev):
    T, D = x1.shape
    FC = FFN_CHUNK
    nch = D_FF // FC
    const = lambda c: (0, 0)
    chunk = lambda c: (0, c)
    out, gate = pl.pallas_call(
        _ffn_step_kernel,
        grid=(nch,),
        in_specs=[pl.BlockSpec((T, D), const), pl.BlockSpec((T, D), const),
                  pl.BlockSpec((D, FC), chunk), pl.BlockSpec((D, FC), lambda c: (0, nch + c)),
                  pl.BlockSpec((FFN_CONV, FC), chunk), pl.BlockSpec((1, FC), chunk),
                  pl.BlockSpec((FC, D), lambda c: (c, 0)),
                  pl.BlockSpec((T, FC), chunk), pl.BlockSpec((T, FC), chunk)],
        out_specs=[pl.BlockSpec((T, D), const), pl.BlockSpec((T, FC), chunk)],
        out_shape=[jax.ShapeDtypeStruct((T, D), F32), jax.ShapeDtypeStruct((T, D_FF), F32)],
        compiler_params=_cparams(("arbitrary",), 32),
        name="ffn_step",
    )(h, x1, w_up, w_up, conv_w, conv_b.reshape(1, D_FF), w_down, prev[:, 0], prev[:, 1])
    return out, jnp.stack([prev[:, 1], gate], axis=1)


def kernel(x_prompt, x_sample, state_gla, cache_sb_k, cache_sb_v, cache_mem_k, cache_mem_v, state_ffn_conv, page_table, mem_prompt, norm_mix, w_in_gla, w_gate_up, b_gate, gla_out_norm, w_in_sb, sb_bias, mem_norm, w_mem_kv, mem_q_norm, mem_k_norm, w_out, norm_ffn, w_ffn_up, ffn_conv_w, ffn_conv_b, w_ffn_down):
    batch, seq, d_model = x_prompt.shape
    nseq = x_sample.shape[0]
    n_mem = mem_prompt.shape[1]
    depth = norm_mix.shape[0]
    gd0 = 2 * GLA_QK + 2 * GLA_VW
    gla_widths = (GLA_QK, GLA_QK, GLA_VW, GLA_VW, MEM_WIDTH)
    sb_widths = (SB_W, SB_W, SB_W, MEM_WIDTH)

    def gla_weights(j):
        w = w_in_gla[j]
        w_main = jnp.concatenate([w[:, :gd0], w[:, gd0 + GLA_GATE_RANK:]], axis=1).astype(BF16)
        w_gd = jnp.pad(w[:, gd0:gd0 + GLA_GATE_RANK], ((0, 0), (0, LANES - GLA_GATE_RANK))).astype(BF16)
        w_gu = jnp.pad(w_gate_up[j], ((0, LANES - GLA_GATE_RANK), (0, 0))).astype(BF16)
        return w_main, (w_gd, w_gu, b_gate[j])

    per_layer_bf16 = lambda w: [w[i].astype(BF16) for i in range(w.shape[0])]
    w_out_b = per_layer_bf16(w_out)
    w_up_b = per_layer_bf16(w_ffn_up)
    w_down_b = per_layer_bf16(w_ffn_down)
    w_kv_b = per_layer_bf16(w_mem_kv)
    w_sb_b = per_layer_bf16(w_in_sb)
    gla_w = [gla_weights(j) for j in range(w_in_gla.shape[0])]

    x = x_prompt.reshape(batch * seq, d_model)
    mem = mem_prompt.reshape(batch * n_mem, d_model)
    p_gla, p_sbk, p_sbv, p_mk, p_mv, p_conv = [], [], [], [], [], []
    for i in range(depth):
        j = i // 2
        if i % 2 == 0:
            w_main, gate = gla_w[j]
            q, k, v, g, mq, la = _proj(x, norm_mix[i], w_main, gla_widths, gate)
            tok, s_fin = _gla_prompt(q, k, la, v, g, gla_out_norm[j], batch, seq)
            p_gla.append(s_fin)
        else:
            q, kt, v, vt, mq = _proj_sb_prompt(x, norm_mix[i], w_sb_b[j], batch, seq)
            tok = _sb_prompt(q, kt, v, sb_bias[j], batch, seq)
            by_pos = lambda t: t.reshape(batch, SB_HEADS, SB_HEAD_DIM, seq).transpose(0, 3, 1, 2)
            p_sbk.append(by_pos(kt))
            p_sbv.append(by_pos(vt))
        mk, mv, mkt, mvb = _mem_kv(mem, mem_norm[i], w_kv_b[i], mem_k_norm[i], n_mem)
        p_mk.append(mk.reshape(batch, n_mem, MEM_HEADS, MEM_HEAD_DIM))
        p_mv.append(mv.reshape(batch, n_mem, MEM_HEADS, MEM_HEAD_DIM))
        x, conv = _tail_prompt(x, tok, mq, mkt, mvb, mem_q_norm[i], w_out_b[i], norm_ffn[i],
                               w_up_b[i], ffn_conv_w[i], ffn_conv_b[i], w_down_b[i], batch, seq)
        p_conv.append(conv)
    y_prompt = x.reshape(batch, seq, d_model)

    dec_seq = x_sample.shape[1]
    x = x_sample.reshape(nseq * dec_seq, d_model)
    s_gla, s_sbk, s_sbv, s_conv = [], [], [], []
    for i in range(depth):
        j = i // 2
        if i % 2 == 0:
            w_main, gate = gla_w[j]
            q, k, v, g, mq, la = _proj(x, norm_mix[i], w_main, gla_widths, gate)
            tok, s_new = _gla_step(q, k, la, v, g, state_gla[j], gla_out_norm[j])
            s_gla.append(s_new)
        else:
            q, k, v, mq = _proj(x, norm_mix[i], w_sb_b[j], sb_widths)
            tok = _sb_decode(q, cache_sb_k, cache_sb_v, j, page_table, sb_bias[j])
            s_sbk.append(k.reshape(nseq, dec_seq, SB_HEADS, SB_HEAD_DIM))
            s_sbv.append(v.reshape(nseq, dec_seq, SB_HEADS, SB_HEAD_DIM))
        mo = _mem_attn_step(mq, cache_mem_k, cache_mem_v, i, mem_q_norm[i])
        x1, h2 = _out_proj(x, tok, mo, w_out_b[i], norm_ffn[i])
        x, conv = _ffn_step(h2, x1, w_up_b[i], ffn_conv_w[i], ffn_conv_b[i], w_down_b[i],
                            state_ffn_conv[i])
        s_conv.append(conv)
    y_sample = x.reshape(nseq, dec_seq, d_model)

    return (y_prompt, y_sample,
            jnp.stack(p_gla), jnp.stack(s_gla),
            jnp.stack(p_sbk), jnp.stack(p_sbv), jnp.stack(s_sbk), jnp.stack(s_sbv),
            jnp.stack(p_mk), jnp.stack(p_mv),
            jnp.stack(p_conv), jnp.stack(s_conv))
```

```python
import functools

import jax
import jax.numpy as jnp
from jax import lax
from jax.experimental import pallas as pl
from jax.experimental.pallas import tpu as pltpu

F32 = jnp.float32
BF16 = jnp.bfloat16

GLA_HEADS = 4
GLA_DK = 64
GLA_DV = 128
GLA_GATE_RANK = 16
GLA_TAU = 16.0
GLA_CHUNK = 64
SB_HEADS = 8
SB_HEAD_DIM = 64
SB_QBLOCK = 128
MEM_HEADS = 4
MEM_HEAD_DIM = 128
D_FF = 2816
FFN_CONV = 3
EPS = 1e-6

GLA_QK = GLA_HEADS * GLA_DK
GLA_VW = GLA_HEADS * GLA_DV
SB_W = SB_HEADS * SB_HEAD_DIM
MEM_WIDTH = MEM_HEADS * MEM_HEAD_DIM

LOG2E = 1.4426950408889634
LANES = 128
FFN_CHUNK = 256
SB_PROMPT_QBLOCK = 2 * SB_QBLOCK
PAGES_PER_STEP = 16


def _cparams(semantics, vmem_mib):
    return pltpu.CompilerParams(dimension_semantics=semantics,
                                vmem_limit_bytes=vmem_mib << 20)


def _dot(a, b):
    return jnp.dot(a, b, preferred_element_type=F32)


def _dot_nt(a, b):
    return lax.dot_general(a, b, (((1,), (1,)), ((), ())), preferred_element_type=F32)


def _rms(xf, g):
    ms = jnp.mean(xf * xf, axis=-1, keepdims=True)
    return xf * lax.rsqrt(ms + EPS) * g


def _log_sigmoid_pair(z):
    t = jnp.log1p(jnp.exp(-jnp.abs(z)))
    return jnp.minimum(z, 0.0) - t, jnp.minimum(-z, 0.0) - t


def _softplus(z):
    return jnp.maximum(z, 0.0) + jnp.log(1.0 + jnp.exp2(jnp.abs(z) * (-LOG2E)))


def _silu(x):
    return x * jax.nn.sigmoid(x)


def _split_bf16(x):
    hi = x.astype(BF16)
    lo = (x - hi.astype(F32)).astype(BF16)
    return hi, lo


def _dot_split(x, m):
    hi, lo = _split_bf16(x)
    return _dot(hi, m) + _dot(lo, m)


def _dot_split_left(m, x):
    hi, lo = _split_bf16(x)
    return _dot(m, hi) + _dot(m, lo)


def _proj_kernel(*refs, widths, gla):
    if gla:
        x_ref, g_ref, w_ref, wgd_ref, wgu_ref, bg_ref = refs[:6]
        outs = refs[6:]
    else:
        x_ref, g_ref, w_ref = refs[:3]
        outs = refs[3:]
    h = _rms(x_ref[...], g_ref[...]).astype(BF16)
    z = _dot(h, w_ref[...])
    off = 0
    for o_ref, wd in zip(outs, widths):
        o_ref[...] = z[:, off:off + wd]
        off += wd
    if gla:
        gd = _dot(h, wgd_ref[...])
        pre = _dot(gd.astype(BF16), wgu_ref[...]) + bg_ref[...]
        ls, _ = _log_sigmoid_pair(pre)
        outs[-1][...] = ls * (1.0 / GLA_TAU)


def _proj(x, gain, w, widths, gate=None):
    T, D = x.shape
    tm = min(T, 512)
    const = lambda i: (0, 0)
    in_specs = [pl.BlockSpec((tm, D), lambda i: (i, 0)),
                pl.BlockSpec((1, D), const),
                pl.BlockSpec(w.shape, const)]
    args = [x, gain.reshape(1, D), w]
    out_widths = list(widths)
    if gate is not None:
        wgd, wgu, bg = gate
        in_specs += [pl.BlockSpec(wgd.shape, const), pl.BlockSpec(wgu.shape, const),
                     pl.BlockSpec((1, GLA_QK), const)]
        args += [wgd, wgu, bg.reshape(1, GLA_QK)]
        out_widths.append(GLA_QK)
    return pl.pallas_call(
        functools.partial(_proj_kernel, widths=tuple(widths), gla=gate is not None),
        grid=(T // tm,),
        in_specs=in_specs,
        out_specs=[pl.BlockSpec((tm, wd), lambda i: (i, 0)) for wd in out_widths],
        out_shape=[jax.ShapeDtypeStruct((T, wd), F32) for wd in out_widths],
        compiler_params=_cparams(("parallel",), 40),
        name="proj_gla" if gate is not None else "proj_sb",
    )(*args)


def _proj_sb_prompt_kernel(x_ref, g_ref, w_ref, q_ref, kt_ref, v_ref, vt_ref, mq_ref):
    h = _rms(x_ref[...], g_ref[...]).astype(BF16)
    z = _dot(h, w_ref[...])
    q_ref[...] = z[:, :SB_W]
    k = z[:, SB_W:2 * SB_W]
    v = z[:, 2 * SB_W:3 * SB_W]
    kt_ref[0] = k.T
    v_ref[...] = v
    vt_ref[0] = v.T
    mq_ref[...] = z[:, 3 * SB_W:]


def _proj_sb_prompt(x, gain, w, batch, seq):
    T, D = x.shape
    tm = 512
    nb = seq // tm
    const = lambda i: (0, 0)
    rows = lambda i: (i, 0)
    cols = lambda i: (i // nb, 0, i % nb)
    row_spec = pl.BlockSpec((tm, SB_W), rows)
    col_spec = pl.BlockSpec((1, SB_W, tm), cols)
    row_shape = jax.ShapeDtypeStruct((T, SB_W), F32)
    col_shape = jax.ShapeDtypeStruct((batch, SB_W, seq), F32)
    return pl.pallas_call(
        _proj_sb_prompt_kernel,
        grid=(T // tm,),
        in_specs=[pl.BlockSpec((tm, D), rows), pl.BlockSpec((1, D), const),
                  pl.BlockSpec(w.shape, const)],
        out_specs=[row_spec, col_spec, row_spec, col_spec, row_spec],
        out_shape=[row_shape, col_shape, row_shape, col_shape, row_shape],
        compiler_params=_cparams(("parallel",), 48),
        name="proj_sb_prompt",
    )(x, gain.reshape(1, D), w)


def _gla_prompt_kernel(q_ref, k_ref, la_ref, v_ref, g_ref, on_ref, tok_ref, s_ref, s_scr):
    i = pl.program_id(1)

    @pl.when(i == 0)
    def _():
        s_scr[...] = jnp.zeros_like(s_scr)

    R = q_ref.shape[0]
    C = GLA_CHUNK
    row = lax.broadcasted_iota(jnp.int32, (R, R), 0)
    col = lax.broadcasted_iota(jnp.int32, (R, R), 1)
    same = (row // C) == (col // C)
    causal = jnp.logical_and(same, col <= row)
    ltri = jnp.where(causal, 1.0, 0.0).astype(BF16)
    ones_bd = jnp.where(same, 1.0, 0.0).astype(BF16)

    la = la_ref[...]
    b = _dot_split_left(ltri, la)
    bt = _dot_split_left(ones_bd, la)
    q_dec = q_ref[...] * (GLA_DK ** -0.5) * jnp.exp(b)
    k = k_ref[...]
    k_dec = k * jnp.exp(-b)
    kte_t = (k * jnp.exp(bt - b)).T
    dec_t = jnp.exp(bt).T

    for h in range(GLA_HEADS):
        ks = slice(h * GLA_DK, (h + 1) * GLA_DK)
        vs = slice(h * GLA_DV, (h + 1) * GLA_DV)
        qh = q_dec[:, ks].astype(BF16)
        kh = k_dec[:, ks].astype(BF16)
        att = jnp.where(causal, _dot_nt(qh, kh), 0.0)
        vh = v_ref[:, vs].astype(BF16)
        o = _dot(att.astype(BF16), vh)
        s = s_scr[h]
        inter = []
        for c in range(R // C):
            rs = slice(c * C, (c + 1) * C)
            inter.append(_dot(qh[rs], s.astype(BF16)))
            kv = _dot(kte_t[ks, rs].astype(BF16), vh[rs])
            s = dec_t[ks, c * C:c * C + 1] * s + kv
        s_scr[h] = s
        o = o + jnp.concatenate(inter, axis=0)
        o = _rms(o, on_ref[...])
        tok_ref[:, vs] = o * _silu(g_ref[:, vs])

    @pl.when(i == pl.num_programs(1) - 1)
    def _():
        s_ref[0] = s_scr[...]


def _gla_prompt(q, k, la, v, g, out_norm, batch, seq):
    R = 256
    nb = seq // R
    rows = lambda b, i: (b * nb + i, 0)
    return pl.pallas_call(
        _gla_prompt_kernel,
        grid=(batch, nb),
        in_specs=[pl.BlockSpec((R, GLA_QK), rows), pl.BlockSpec((R, GLA_QK), rows),
                  pl.BlockSpec((R, GLA_QK), rows), pl.BlockSpec((R, GLA_VW), rows),
                  pl.BlockSpec((R, GLA_VW), rows),
                  pl.BlockSpec((1, GLA_DV), lambda b, i: (0, 0))],
        out_specs=[pl.BlockSpec((R, GLA_VW), rows),
                   pl.BlockSpec((1, GLA_HEADS, GLA_DK, GLA_DV), lambda b, i: (b, 0, 0, 0))],
        out_shape=[jax.ShapeDtypeStruct((batch * seq, GLA_VW), F32),
                   jax.ShapeDtypeStruct((batch, GLA_HEADS, GLA_DK, GLA_DV), F32)],
        scratch_shapes=[pltpu.VMEM((GLA_HEADS, GLA_DK, GLA_DV), F32)],
        compiler_params=_cparams(("parallel", "arbitrary"), 32),
        name="gla_prompt",
    )(q, k, la, v, g, out_norm.reshape(1, GLA_DV))


def _gla_step_kernel(qt_ref, kt_ref, lat_ref, v_ref, g_ref, s0_ref, on_ref, tok_ref, sn_ref):
    nb = v_ref.shape[0]
    for i in range(nb):
        for h in range(GLA_HEADS):
            vs = slice(h * GLA_DV, (h + 1) * GLA_DV)
            qc = qt_ref[i, :, h:h + 1] * (GLA_DK ** -0.5)
            kc = kt_ref[i, :, h:h + 1]
            ac = jnp.exp(lat_ref[i, :, h:h + 1])
            s = ac * s0_ref[i, h] + kc * v_ref[i:i + 1, vs]
            sn_ref[i, h] = s
            o = jnp.sum(qc * s, axis=0, keepdims=True)
            o = _rms(o, on_ref[...])
            tok_ref[i:i + 1, vs] = o * _silu(g_ref[i:i + 1, vs])


def _gla_step(q, k, la, v, g, s0, out_norm):
    nseq = q.shape[0]
    nb = 8
    cols = lambda a: a.reshape(nseq, GLA_HEADS, GLA_DK).transpose(0, 2, 1)
    col_spec = pl.BlockSpec((nb, GLA_DK, GLA_HEADS), lambda i: (i, 0, 0))
    row_spec = pl.BlockSpec((nb, GLA_VW), lambda i: (i, 0))
    st_spec = pl.BlockSpec((nb, GLA_HEADS, GLA_DK, GLA_DV), lambda i: (i, 0, 0, 0))
    return pl.pallas_call(
        _gla_step_kernel,
        grid=(nseq // nb,),
        in_specs=[col_spec, col_spec, col_spec, row_spec, row_spec, st_spec,
                  pl.BlockSpec((1, GLA_DV), lambda i: (0, 0))],
        out_specs=[row_spec, st_spec],
        out_shape=[jax.ShapeDtypeStruct((nseq, GLA_VW), F32),
                   jax.ShapeDtypeStruct(s0.shape, F32)],
        compiler_params=_cparams(("parallel",), 32),
        name="gla_step",
    )(cols(q), cols(k), cols(la), v, g, s0, out_norm.reshape(1, GLA_DV))


def _sb_prompt_kernel(bias_ref, q_ref, kt_ref, v_ref, o_ref, qq_scr, c_scr, acc_scr, *, nq):
    hp = pl.program_id(1)
    i = pl.program_id(2)
    QB = SB_PROMPT_QBLOCK
    lane = lax.broadcasted_iota(jnp.int32, (QB, LANES), 1)
    first = lane < SB_HEAD_DIM
    rowi = lax.broadcasted_iota(jnp.int32, (2 * QB, QB), 0)
    coli = lax.broadcasted_iota(jnp.int32, (2 * QB, QB), 1)
    r1 = lax.broadcasted_iota(jnp.int32, (QB, QB), 0)
    c1 = lax.broadcasted_iota(jnp.int32, (QB, QB), 1)
    later = jnp.where(r1 > c1, 1.0, 0.0).astype(BF16)
    strict = coli < jnp.bitwise_and(rowi, QB - 1)
    b0 = bias_ref[2 * hp]
    b1 = bias_ref[2 * hp + 1]

    def block(kb, qq, c, masked):
        start = pl.multiple_of(kb * QB, QB)
        kk = kt_ref[0, :, pl.ds(start, QB)].astype(BF16)
        vv = v_ref[pl.ds(start, QB), :].astype(BF16)
        z = _dot(qq, kk)
        z = jnp.concatenate([z[:QB] + b0, z[QB:] + b1], axis=0)
        nlf = _softplus(z)
        if masked:
            nlf = jnp.where(strict, nlf, 0.0)
        after = _dot(nlf.astype(BF16), later) + jnp.concatenate([c] * (QB // LANES), axis=1)
        w = jnp.exp((z - nlf) - after)
        if masked:
            w = jnp.where(strict, w, 0.0)
        return c + jnp.sum(nlf, axis=1, keepdims=True), _dot(w.astype(BF16), vv)

    qblocks = (i, nq - 1 - i)
    for sel, qi in enumerate(qblocks):
        q = q_ref[pl.ds(pl.multiple_of(qi * QB, QB), QB), :] * (SB_HEAD_DIM ** -0.5)
        qq = jnp.concatenate([jnp.where(first, q, 0.0), jnp.where(first, 0.0, q)],
                             axis=0).astype(BF16)
        qq_scr[sel] = qq
        c, pv = block(qi, qq, jnp.zeros((2 * QB, LANES), F32), True)
        c_scr[sel] = c
        acc_scr[sel] = pv

    for t in range(nq - 1):
        sel = jnp.where(t < i, 0, 1)
        kb = jnp.where(t < i, i - 1 - t, nq - 2 - t)
        c, pv = block(kb, qq_scr[sel], c_scr[sel], False)
        c_scr[sel] = c
        acc_scr[sel] += pv

    for sel, qi in enumerate(qblocks):
        o_ref[pl.ds(pl.multiple_of(qi * QB, QB), QB), :] = jnp.where(
            first, acc_scr[sel, :QB], acc_scr[sel, QB:])


def _sb_prompt(q, kt, v, bias, batch, seq):
    QB = SB_PROMPT_QBLOCK
    nq = seq // QB
    npair = SB_W // LANES
    rowspec = pl.BlockSpec((seq, LANES), lambda b, hp, i: (b, hp))
    ktspec = pl.BlockSpec((1, LANES, seq), lambda b, hp, i: (b, hp, 0))
    return pl.pallas_call(
        functools.partial(_sb_prompt_kernel, nq=nq),
        grid=(batch, npair, nq // 2),
        in_specs=[pl.BlockSpec(memory_space=pltpu.SMEM), rowspec, ktspec, rowspec],
        out_specs=rowspec,
        out_shape=jax.ShapeDtypeStruct((batch * seq, SB_W), F32),
        scratch_shapes=[pltpu.VMEM((2, 2 * QB, LANES), BF16), pltpu.VMEM((2, 2 * QB, LANES), F32),
                        pltpu.VMEM((2, 2 * QB, LANES), F32)],
        compiler_params=_cparams(("parallel", "parallel", "arbitrary"), 32),
        name="sb_prompt",
    )(bias, q, kt, v)


def _sb_decode_kernel(pt_ref, bias_ref, q_ref, *rest):
    NP = PAGES_PER_STEP
    k_refs = rest[:NP]
    v_refs = rest[NP:2 * NP]
    o_ref = rest[2 * NP]
    acc_scr, carry_scr = rest[2 * NP + 1:]
    s = pl.program_id(1)

    @pl.when(s == 0)
    def _():
        acc_scr[...] = jnp.zeros_like(acc_scr)
        carry_scr[...] = jnp.zeros_like(carry_scr)

    H = SB_HEADS
    D = SB_HEAD_DIM
    P = k_refs[0].shape[4]
    sub = lax.broadcasted_iota(jnp.int32, (H, P), 0)
    bias = jnp.zeros((H, P), F32)
    for h in range(H):
        bias = jnp.where(sub == h, bias_ref[h], bias)
    rowi = lax.broadcasted_iota(jnp.int32, (P, P), 0)
    coli = lax.broadcasted_iota(jnp.int32, (P, P), 1)
    later = jnp.where(rowi > coli, 1.0, 0.0).astype(BF16)

    zs = []
    for p in range(NP):
        zp = bias
        for h in range(H):
            zh = jnp.sum(k_refs[p][0, 0, h] * q_ref[0, h], axis=0, keepdims=True)
            zp = jnp.where(sub == h, zh * (D ** -0.5) + bias, zp)
        zs.append(zp)
    z = jnp.concatenate(zs, axis=0)
    nlf = _softplus(z)
    after = _dot(nlf.astype(BF16), later)
    run = carry_scr[...]
    ws = [None] * NP
    for p in reversed(range(NP)):
        rs = slice(p * H, (p + 1) * H)
        ws[p] = jnp.exp((z[rs] - nlf[rs]) - (after[rs] + run))
        run = run + jnp.sum(nlf[rs], axis=1, keepdims=True)
    carry_scr[...] = run
    for h in range(H):
        a = acc_scr[h]
        for p in range(NP):
            a = a + v_refs[p][0, 0, h] * ws[p][h:h + 1, :]
        acc_scr[h] = a

    @pl.when(s == pl.num_programs(1) - 1)
    def _():
        ones = jnp.ones((H, P), BF16)
        subd = lax.broadcasted_iota(jnp.int32, (H, D), 0)
        out = jnp.zeros((H, D), F32)
        for h in range(H):
            hi, lo = _split_bf16(acc_scr[h])
            out = jnp.where(subd == h, _dot_nt(ones, hi) + _dot_nt(ones, lo), out)
        o_ref[0] = out


def _sb_decode(q, cache_k, cache_v, layer, page_table, bias):
    nseq = q.shape[0]
    n_pages = page_table.shape[1]
    P = cache_k.shape[2]
    NP = PAGES_PER_STEP
    nsteps = n_pages // NP
    kt = jnp.transpose(cache_k, (0, 1, 3, 4, 2))
    vt = jnp.transpose(cache_v, (0, 1, 3, 4, 2))
    qb = jnp.broadcast_to(q.reshape(nseq, SB_HEADS, SB_HEAD_DIM, 1),
                          (nseq, SB_HEADS, SB_HEAD_DIM, P))

    def page_spec(p):
        return pl.BlockSpec((1, 1, SB_HEADS, SB_HEAD_DIM, P),
                            lambda b, s, pt: (layer, pt[b, (nsteps - 1 - s) * NP + p], 0, 0, 0))

    grid_spec = pltpu.PrefetchScalarGridSpec(
        num_scalar_prefetch=1,
        grid=(nseq, nsteps),
        in_specs=[pl.BlockSpec(memory_space=pltpu.SMEM),
                  pl.BlockSpec((1, SB_HEADS, SB_HEAD_DIM, P), lambda b, s, pt: (b, 0, 0, 0))]
        + [page_spec(p) for p in range(NP)] * 2,
        out_specs=pl.BlockSpec((1, SB_HEADS, SB_HEAD_DIM), lambda b, s, pt: (b, 0, 0)),
        scratch_shapes=[pltpu.VMEM((SB_HEADS, SB_HEAD_DIM, P), F32),
                        pltpu.VMEM((SB_HEADS, P), F32)],
    )
    out = pl.pallas_call(
        _sb_decode_kernel,
        grid_spec=grid_spec,
        out_shape=jax.ShapeDtypeStruct((nseq, SB_HEADS, SB_HEAD_DIM), F32),
        compiler_params=_cparams(("parallel", "arbitrary"), 32),
        name="sb_decode",
    )(page_table, bias, qb, *([kt] * NP), *([vt] * NP))
    return out.reshape(nseq, SB_W)


def _mem_kv_kernel(m_ref, g_ref, w_ref, kn_ref, mk_ref, mv_ref, mkt_ref, mvb_ref):
    h = _rms(m_ref[...], g_ref[...]).astype(BF16)
    kv = _dot(h, w_ref[...])
    for hd in range(MEM_HEADS):
        hs = slice(hd * MEM_HEAD_DIM, (hd + 1) * MEM_HEAD_DIM)
        mk = _rms(kv[:, hs], kn_ref[...])
        mk_ref[:, hs] = mk
        mkt_ref[0, hs, :] = mk.T.astype(BF16)
    mv = kv[:, MEM_WIDTH:]
    mv_ref[...] = mv
    mvb_ref[...] = mv.astype(BF16)


def _mem_kv(mem, gain, w, k_norm, n_mem):
    T, D = mem.shape
    const = lambda i: (0, 0)
    rows = lambda i: (i, 0)
    return pl.pallas_call(
        _mem_kv_kernel,
        grid=(T // n_mem,),
        in_specs=[pl.BlockSpec((n_mem, D), rows), pl.BlockSpec((1, D), const),
                  pl.BlockSpec(w.shape, const), pl.BlockSpec((1, MEM_HEAD_DIM), const)],
        out_specs=[pl.BlockSpec((n_mem, MEM_WIDTH), rows), pl.BlockSpec((n_mem, MEM_WIDTH), rows),
                   pl.BlockSpec((1, MEM_WIDTH, n_mem), lambda i: (i, 0, 0)),
                   pl.BlockSpec((n_mem, MEM_WIDTH), rows)],
        out_shape=[jax.ShapeDtypeStruct((T, MEM_WIDTH), F32), jax.ShapeDtypeStruct((T, MEM_WIDTH), F32),
                   jax.ShapeDtypeStruct((T // n_mem, MEM_WIDTH, n_mem), BF16),
                   jax.ShapeDtypeStruct((T, MEM_WIDTH), BF16)],
        compiler_params=_cparams(("parallel",), 32),
        name="mem_kv",
    )(mem, gain.reshape(1, D), w, k_norm.reshape(1, MEM_HEAD_DIM))


def _mem_attn_step_kernel(q_ref, mk_ref, mv_ref, qn_ref, o_ref):
    for i in range(q_ref.shape[0]):
        q = _rms(q_ref[i], qn_ref[...])
        s = jnp.sum(mk_ref[i] * q, axis=-1, keepdims=True) * (MEM_HEAD_DIM ** -0.5)
        e = jnp.exp(s - jnp.max(s, axis=0, keepdims=True))
        p = e / jnp.sum(e, axis=0, keepdims=True)
        o_ref[i] = jnp.sum(p * mv_ref[i], axis=0)


def _mem_attn_step(mq, cache_k, cache_v, layer, q_norm):
    nseq = mq.shape[0]
    n_mem = cache_k.shape[2]
    nb = 8
    qspec = pl.BlockSpec((nb, MEM_HEADS, MEM_HEAD_DIM), lambda i: (i, 0, 0))
    kvspec = pl.BlockSpec((None, nb, n_mem, MEM_HEADS, MEM_HEAD_DIM), lambda i: (layer, i, 0, 0, 0))
    out = pl.pallas_call(
        _mem_attn_step_kernel,
        grid=(nseq // nb,),
        in_specs=[qspec, kvspec, kvspec, pl.BlockSpec((1, MEM_HEAD_DIM), lambda i: (0, 0))],
        out_specs=qspec,
        out_shape=jax.ShapeDtypeStruct((nseq, MEM_HEADS, MEM_HEAD_DIM), F32),
        compiler_params=_cparams(("parallel",), 32),
        name="mem_attn_step",
    )(mq.reshape(nseq, MEM_HEADS, MEM_HEAD_DIM), cache_k, cache_v, q_norm.reshape(1, MEM_HEAD_DIM))
    return out.reshape(nseq, MEM_WIDTH)


def _out_proj_kernel(x_ref, tok_ref, mo_ref, w_ref, g_ref, x1_ref, h_ref):
    tw = tok_ref.shape[1]
    y = _dot(tok_ref[...].astype(BF16), w_ref[:tw, :]) + _dot(mo_ref[...].astype(BF16), w_ref[tw:, :])
    x1 = x_ref[...] + y
    x1_ref[...] = x1
    h_ref[...] = _rms(x1, g_ref[...]).astype(BF16)


def _out_proj(x, tok, mo, w, gain):
    T, D = x.shape
    tm = min(T, 256)
    const = lambda i: (0, 0)
    rows = lambda i: (i, 0)
    return pl.pallas_call(
        _out_proj_kernel,
        grid=(T // tm,),
        in_specs=[pl.BlockSpec((tm, D), rows), pl.BlockSpec((tm, tok.shape[1]), rows),
                  pl.BlockSpec((tm, mo.shape[1]), rows), pl.BlockSpec(w.shape, const),
                  pl.BlockSpec((1, D), const)],
        out_specs=[pl.BlockSpec((tm, D), rows)] * 2,
        out_shape=[jax.ShapeDtypeStruct((T, D), F32), jax.ShapeDtypeStruct((T, D), BF16)],
        compiler_params=_cparams(("parallel",), 32),
        name="out_proj",
    )(x, tok, mo, w, gain.reshape(1, D))


def _tail_prompt_kernel(x_ref, tok_ref, mq_ref, mkt_ref, mv_ref, qn_ref, wo_ref, g_ref,
                        wu_ref, cw_ref, cb_ref, wd_ref, o_ref, cs_ref, carry_scr, h_scr, act_scr):
    i = pl.program_id(1)

    @pl.when(i == 0)
    def _():
        carry_scr[...] = jnp.zeros_like(carry_scr)

    tm = x_ref.shape[0]
    tw = tok_ref.shape[1]
    y = _dot(tok_ref[...].astype(BF16), wo_ref[:tw, :])
    for hd in range(MEM_HEADS):
        hs = slice(hd * MEM_HEAD_DIM, (hd + 1) * MEM_HEAD_DIM)
        qh = _rms(mq_ref[:, hs], qn_ref[...]).astype(BF16)
        s = _dot(qh, mkt_ref[0, hs, :]) * (MEM_HEAD_DIM ** -0.5)
        e = jnp.exp(s - jnp.max(s, axis=-1, keepdims=True))
        o = _dot(e.astype(BF16), mv_ref[:, hs]) / jnp.sum(e, axis=-1, keepdims=True)
        y = y + _dot(o.astype(BF16), wo_ref[tw + hd * MEM_HEAD_DIM:tw + (hd + 1) * MEM_HEAD_DIM, :])
    x1 = x_ref[...] + y
    o_ref[...] = x1
    h_scr[...] = _rms(x1, g_ref[...]).astype(BF16)

    FC = FFN_CHUNK
    h = h_scr[...]
    rowi = lax.broadcasted_iota(jnp.int32, (tm, FC), 0)
    for c in range(D_FF // FC):
        cs = slice(c * FC, (c + 1) * FC)
        gate = _dot(h, wu_ref[:, cs])
        val = _dot(h, wu_ref[:, D_FF + c * FC:D_FF + (c + 1) * FC])
        p0 = carry_scr[0:1, cs]
        p1 = carry_scr[1:2, cs]
        g1 = jnp.where(rowi == 0, p1, pltpu.roll(gate, 1, 0))
        g2 = jnp.where(rowi == 0, p0, jnp.where(rowi == 1, p1, pltpu.roll(gate, 2, 0)))
        a = cb_ref[:, cs] + g2 * cw_ref[0:1, cs] + g1 * cw_ref[1:2, cs] + gate * cw_ref[2:3, cs]
        act_scr[:, cs] = (_silu(a) * val).astype(BF16)
        last = gate[tm - (FFN_CONV - 1):, :]
        carry_scr[0:FFN_CONV - 1, cs] = last
        cs_ref[0, :, cs] = last
    o_ref[...] += _dot(act_scr[...], wd_ref[...])


def _tail_prompt(x, tok, mq, mkt, mvb, q_norm, w_o, gain, w_up, conv_w, conv_b, w_down, layer,
                 batch, seq):
    T, D = x.shape
    this_layer = lambda b, i: (layer, 0, 0)
    n_mem = mkt.shape[2]
    tm = 512
    nb = seq // tm
    rows = lambda b, i: (b * nb + i, 0)
    const = lambda b, i: (0, 0)
    once = pl.Buffered(1)
    return pl.pallas_call(
        _tail_prompt_kernel,
        grid=(batch, nb),
        in_specs=[pl.BlockSpec((tm, D), rows), pl.BlockSpec((tm, tok.shape[1]), rows),
                  pl.BlockSpec((tm, MEM_WIDTH), rows),
                  pl.BlockSpec((1, MEM_WIDTH, n_mem), lambda b, i: (b, 0, 0)),
                  pl.BlockSpec((n_mem, MEM_WIDTH), lambda b, i: (b, 0)),
                  pl.BlockSpec((1, MEM_HEAD_DIM), const),
                  pl.BlockSpec(w_o.shape, const, pipeline_mode=once),
                  pl.BlockSpec((1, D), const),
                  pl.BlockSpec((None,) + w_up.shape[1:], this_layer, pipeline_mode=once),
                  pl.BlockSpec(conv_w.shape, const), pl.BlockSpec((1, D_FF), const),
                  pl.BlockSpec((None,) + w_down.shape[1:], this_layer, pipeline_mode=once)],
        out_specs=[pl.BlockSpec((tm, D), rows),
                   pl.BlockSpec((1, FFN_CONV - 1, D_FF), lambda b, i: (b, 0, 0))],
        out_shape=[jax.ShapeDtypeStruct((T, D), F32),
                   jax.ShapeDtypeStruct((batch, FFN_CONV - 1, D_FF), F32)],
        scratch_shapes=[pltpu.VMEM((8, D_FF), F32), pltpu.VMEM((tm, D), BF16),
                        pltpu.VMEM((tm, D_FF), BF16)],
        compiler_params=_cparams(("parallel", "arbitrary"), 56),
        name="tail_prompt",
    )(x, tok, mq, mkt, mvb, q_norm.reshape(1, MEM_HEAD_DIM), w_o, gain.reshape(1, D),
      w_up, conv_w, conv_b.reshape(1, D_FF), w_down)


def _ffn_step_kernel(h_ref, x_ref, wg_ref, wv_ref, cw_ref, cb_ref, wd_ref, p0_ref, p1_ref,
                     o_ref, gate_ref):
    @pl.when(pl.program_id(0) == 0)
    def _():
        o_ref[...] = x_ref[...]

    h = h_ref[...]
    gate = _dot(h, wg_ref[...])
    val = _dot(h, wv_ref[...])
    a = (cb_ref[...] + p0_ref[...] * cw_ref[0:1, :] + p1_ref[...] * cw_ref[1:2, :]
         + gate * cw_ref[2:3, :])
    o_ref[...] += _dot((_silu(a) * val).astype(BF16), wd_ref[...])
    gate_ref[...] = gate


def _ffn_step(h, x1, w_up, conv_w, conv_b, w_down, layer, prev):
    T, D = x1.shape
    FC = FFN_CHUNK
    nch = D_FF // FC
    const = lambda c: (0, 0)
    chunk = lambda c: (0, c)
    out, gate = pl.pallas_call(
        _ffn_step_kernel,
        grid=(nch,),
        in_specs=[pl.BlockSpec((T, D), const), pl.BlockSpec((T, D), const),
                  pl.BlockSpec((None, D, FC), lambda c: (layer, 0, c)),
                  pl.BlockSpec((None, D, FC), lambda c: (layer, 0, nch + c)),
                  pl.BlockSpec((FFN_CONV, FC), chunk), pl.BlockSpec((1, FC), chunk),
                  pl.BlockSpec((None, FC, D), lambda c: (layer, c, 0)),
                  pl.BlockSpec((T, FC), chunk), pl.BlockSpec((T, FC), chunk)],
        out_specs=[pl.BlockSpec((T, D), const), pl.BlockSpec((T, FC), chunk)],
        out_shape=[jax.ShapeDtypeStruct((T, D), F32), jax.ShapeDtypeStruct((T, D_FF), F32)],
        compiler_params=_cparams(("arbitrary",), 32),
        name="ffn_step",
    )(h, x1, w_up, w_up, conv_w, conv_b.reshape(1, D_FF), w_down, prev[:, 0], prev[:, 1])
    return out, jnp.stack([prev[:, 1], gate], axis=1)


def kernel(x_prompt, x_sample, state_gla, cache_sb_k, cache_sb_v, cache_mem_k, cache_mem_v, state_ffn_conv, page_table, mem_prompt, norm_mix, w_in_gla, w_gate_up, b_gate, gla_out_norm, w_in_sb, sb_bias, mem_norm, w_mem_kv, mem_q_norm, mem_k_norm, w_out, norm_ffn, w_ffn_up, ffn_conv_w, ffn_conv_b, w_ffn_down):
    batch, seq, d_model = x_prompt.shape
    nseq = x_sample.shape[0]
    n_mem = mem_prompt.shape[1]
    depth = norm_mix.shape[0]
    gd0 = 2 * GLA_QK + 2 * GLA_VW
    gla_widths = (GLA_QK, GLA_QK, GLA_VW, GLA_VW, MEM_WIDTH)
    sb_widths = (SB_W, SB_W, SB_W, MEM_WIDTH)

    def gla_weights(j):
        w = w_in_gla[j]
        w_main = jnp.concatenate([w[:, :gd0], w[:, gd0 + GLA_GATE_RANK:]], axis=1).astype(BF16)
        w_gd = jnp.pad(w[:, gd0:gd0 + GLA_GATE_RANK], ((0, 0), (0, LANES - GLA_GATE_RANK))).astype(BF16)
        w_gu = jnp.pad(w_gate_up[j], ((0, LANES - GLA_GATE_RANK), (0, 0))).astype(BF16)
        return w_main, (w_gd, w_gu, b_gate[j])

    per_layer_bf16 = lambda w: [w[i].astype(BF16) for i in range(w.shape[0])]
    w_out_b = per_layer_bf16(w_out)
    w_up_b = w_ffn_up.astype(BF16)
    w_down_b = w_ffn_down.astype(BF16)
    w_kv_b = per_layer_bf16(w_mem_kv)
    w_sb_b = per_layer_bf16(w_in_sb)
    gla_w = [gla_weights(j) for j in range(w_in_gla.shape[0])]

    x = x_prompt.reshape(batch * seq, d_model)
    mem = mem_prompt.reshape(batch * n_mem, d_model)
    p_gla, p_sbk, p_sbv, p_mk, p_mv, p_conv = [], [], [], [], [], []
    for i in range(depth):
        j = i // 2
        if i % 2 == 0:
            w_main, gate = gla_w[j]
            q, k, v, g, mq, la = _proj(x, norm_mix[i], w_main, gla_widths, gate)
            tok, s_fin = _gla_prompt(q, k, la, v, g, gla_out_norm[j], batch, seq)
            p_gla.append(s_fin)
        else:
            q, kt, v, vt, mq = _proj_sb_prompt(x, norm_mix[i], w_sb_b[j], batch, seq)
            tok = _sb_prompt(q, kt, v, sb_bias[j], batch, seq)
            by_pos = lambda t: t.reshape(batch, SB_HEADS, SB_HEAD_DIM, seq).transpose(0, 3, 1, 2)
            p_sbk.append(by_pos(kt))
            p_sbv.append(by_pos(vt))
        mk, mv, mkt, mvb = _mem_kv(mem, mem_norm[i], w_kv_b[i], mem_k_norm[i], n_mem)
        p_mk.append(mk.reshape(batch, n_mem, MEM_HEADS, MEM_HEAD_DIM))
        p_mv.append(mv.reshape(batch, n_mem, MEM_HEADS, MEM_HEAD_DIM))
        x, conv = _tail_prompt(x, tok, mq, mkt, mvb, mem_q_norm[i], w_out_b[i], norm_ffn[i],
                               w_up_b, ffn_conv_w[i], ffn_conv_b[i], w_down_b, i, batch, seq)
        p_conv.append(conv)
    y_prompt = x.reshape(batch, seq, d_model)

    dec_seq = x_sample.shape[1]
    x = x_sample.reshape(nseq * dec_seq, d_model)
    s_gla, s_sbk, s_sbv, s_conv = [], [], [], []
    for i in range(depth):
        j = i // 2
        if i % 2 == 0:
            w_main, gate = gla_w[j]
            q, k, v, g, mq, la = _proj(x, norm_mix[i], w_main, gla_widths, gate)
            tok, s_new = _gla_step(q, k, la, v, g, state_gla[j], gla_out_norm[j])
            s_gla.append(s_new)
        else:
            q, k, v, mq = _proj(x, norm_mix[i], w_sb_b[j], sb_widths)
            tok = _sb_decode(q, cache_sb_k, cache_sb_v, j, page_table, sb_bias[j])
            s_sbk.append(k.reshape(nseq, dec_seq, SB_HEADS, SB_HEAD_DIM))
            s_sbv.append(v.reshape(nseq, dec_seq, SB_HEADS, SB_HEAD_DIM))
        mo = _mem_attn_step(mq, cache_mem_k, cache_mem_v, i, mem_q_norm[i])
        x1, h2 = _out_proj(x, tok, mo, w_out_b[i], norm_ffn[i])
        x, conv = _ffn_step(h2, x1, w_up_b, ffn_conv_w[i], ffn_conv_b[i], w_down_b, i,
                            state_ffn_conv[i])
        s_conv.append(conv)
    y_sample = x.reshape(nseq, dec_seq, d_model)

    return (y_prompt, y_sample,
            jnp.stack(p_gla), jnp.stack(s_gla),
            jnp.stack(p_sbk), jnp.stack(p_sbv), jnp.stack(s_sbk), jnp.stack(s_sbv),
            jnp.stack(p_mk), jnp.stack(p_mv),
            jnp.stack(p_conv), jnp.stack(s_conv))
```

```python
import functools

import jax
import jax.numpy as jnp
from jax import lax
from jax.experimental import pallas as pl
from jax.experimental.pallas import tpu as pltpu

F32 = jnp.float32
BF16 = jnp.bfloat16

GLA_HEADS = 4
GLA_DK = 64
GLA_DV = 128
GLA_GATE_RANK = 16
GLA_TAU = 16.0
GLA_CHUNK = 64
SB_HEADS = 8
SB_HEAD_DIM = 64
SB_QBLOCK = 128
MEM_HEADS = 4
MEM_HEAD_DIM = 128
D_FF = 2816
FFN_CONV = 3
EPS = 1e-6

GLA_QK = GLA_HEADS * GLA_DK
GLA_VW = GLA_HEADS * GLA_DV
SB_W = SB_HEADS * SB_HEAD_DIM
MEM_WIDTH = MEM_HEADS * MEM_HEAD_DIM

LOG2E = 1.4426950408889634
LANES = 128
FFN_CHUNK = 256
SB_PROMPT_QBLOCK = 2 * SB_QBLOCK
DECODE_CHUNK = 8


def _cparams(semantics, vmem_mib):
    return pltpu.CompilerParams(dimension_semantics=semantics,
                                vmem_limit_bytes=vmem_mib << 20)


def _dot(a, b):
    return jnp.dot(a, b, preferred_element_type=F32)


def _dot_nt(a, b):
    return lax.dot_general(a, b, (((1,), (1,)), ((), ())), preferred_element_type=F32)


def _rms(xf, g):
    ms = jnp.mean(xf * xf, axis=-1, keepdims=True)
    return xf * lax.rsqrt(ms + EPS) * g


def _log_sigmoid_pair(z):
    t = jnp.log1p(jnp.exp(-jnp.abs(z)))
    return jnp.minimum(z, 0.0) - t, jnp.minimum(-z, 0.0) - t


def _softplus(z):
    return jnp.maximum(z, 0.0) + jnp.log(1.0 + jnp.exp2(jnp.abs(z) * (-LOG2E)))


def _silu(x):
    return x * jax.nn.sigmoid(x)


def _split_bf16(x):
    hi = x.astype(BF16)
    lo = (x - hi.astype(F32)).astype(BF16)
    return hi, lo


def _dot_split(x, m):
    hi, lo = _split_bf16(x)
    return _dot(hi, m) + _dot(lo, m)


def _dot_split_left(m, x):
    hi, lo = _split_bf16(x)
    return _dot(m, hi) + _dot(m, lo)


def _proj_kernel(*refs, widths, gla):
    if gla:
        x_ref, g_ref, w_ref, wgd_ref, wgu_ref, bg_ref = refs[:6]
        outs = refs[6:]
    else:
        x_ref, g_ref, w_ref = refs[:3]
        outs = refs[3:]
    h = _rms(x_ref[...], g_ref[...]).astype(BF16)
    z = _dot(h, w_ref[...])
    off = 0
    for o_ref, wd in zip(outs, widths):
        o_ref[...] = z[:, off:off + wd]
        off += wd
    if gla:
        gd = _dot(h, wgd_ref[...])
        pre = _dot(gd.astype(BF16), wgu_ref[...]) + bg_ref[...]
        ls, _ = _log_sigmoid_pair(pre)
        outs[-1][...] = ls * (1.0 / GLA_TAU)


def _proj(x, gain, w, widths, gate=None):
    T, D = x.shape
    tm = min(T, 512)
    const = lambda i: (0, 0)
    in_specs = [pl.BlockSpec((tm, D), lambda i: (i, 0)),
                pl.BlockSpec((1, D), const),
                pl.BlockSpec(w.shape, const)]
    args = [x, gain.reshape(1, D), w]
    out_widths = list(widths)
    if gate is not None:
        wgd, wgu, bg = gate
        in_specs += [pl.BlockSpec(wgd.shape, const), pl.BlockSpec(wgu.shape, const),
                     pl.BlockSpec((1, GLA_QK), const)]
        args += [wgd, wgu, bg.reshape(1, GLA_QK)]
        out_widths.append(GLA_QK)
    return pl.pallas_call(
        functools.partial(_proj_kernel, widths=tuple(widths), gla=gate is not None),
        grid=(T // tm,),
        in_specs=in_specs,
        out_specs=[pl.BlockSpec((tm, wd), lambda i: (i, 0)) for wd in out_widths],
        out_shape=[jax.ShapeDtypeStruct((T, wd), F32) for wd in out_widths],
        compiler_params=_cparams(("parallel",), 40),
        name="proj_gla" if gate is not None else "proj_sb",
    )(*args)


def _proj_sb_prompt_kernel(x_ref, g_ref, w_ref, q_ref, kt_ref, v_ref, vt_ref, mq_ref):
    h = _rms(x_ref[...], g_ref[...]).astype(BF16)
    z = _dot(h, w_ref[...])
    q_ref[...] = z[:, :SB_W]
    k = z[:, SB_W:2 * SB_W]
    v = z[:, 2 * SB_W:3 * SB_W]
    kt_ref[0] = k.T
    v_ref[...] = v
    vt_ref[0] = v.T
    mq_ref[...] = z[:, 3 * SB_W:]


def _proj_sb_prompt(x, gain, w, batch, seq):
    T, D = x.shape
    tm = 512
    nb = seq // tm
    const = lambda i: (0, 0)
    rows = lambda i: (i, 0)
    cols = lambda i: (i // nb, 0, i % nb)
    row_spec = pl.BlockSpec((tm, SB_W), rows)
    col_spec = pl.BlockSpec((1, SB_W, tm), cols)
    row_shape = jax.ShapeDtypeStruct((T, SB_W), F32)
    col_shape = jax.ShapeDtypeStruct((batch, SB_W, seq), F32)
    return pl.pallas_call(
        _proj_sb_prompt_kernel,
        grid=(T // tm,),
        in_specs=[pl.BlockSpec((tm, D), rows), pl.BlockSpec((1, D), const),
                  pl.BlockSpec(w.shape, const)],
        out_specs=[row_spec, col_spec, row_spec, col_spec, row_spec],
        out_shape=[row_shape, col_shape, row_shape, col_shape, row_shape],
        compiler_params=_cparams(("parallel",), 48),
        name="proj_sb_prompt",
    )(x, gain.reshape(1, D), w)


def _gla_prompt_kernel(q_ref, k_ref, la_ref, v_ref, g_ref, on_ref, tok_ref, s_ref, s_scr):
    i = pl.program_id(1)

    @pl.when(i == 0)
    def _():
        s_scr[...] = jnp.zeros_like(s_scr)

    R = q_ref.shape[0]
    C = GLA_CHUNK
    row = lax.broadcasted_iota(jnp.int32, (R, R), 0)
    col = lax.broadcasted_iota(jnp.int32, (R, R), 1)
    same = (row // C) == (col // C)
    causal = jnp.logical_and(same, col <= row)
    ltri = jnp.where(causal, 1.0, 0.0).astype(BF16)
    ones_bd = jnp.where(same, 1.0, 0.0).astype(BF16)

    la = la_ref[...]
    b = _dot_split_left(ltri, la)
    bt = _dot_split_left(ones_bd, la)
    q_dec = q_ref[...] * (GLA_DK ** -0.5) * jnp.exp(b)
    k = k_ref[...]
    k_dec = k * jnp.exp(-b)
    kte_t = (k * jnp.exp(bt - b)).T
    dec_t = jnp.exp(bt).T

    for h in range(GLA_HEADS):
        ks = slice(h * GLA_DK, (h + 1) * GLA_DK)
        vs = slice(h * GLA_DV, (h + 1) * GLA_DV)
        qh = q_dec[:, ks].astype(BF16)
        kh = k_dec[:, ks].astype(BF16)
        att = jnp.where(causal, _dot_nt(qh, kh), 0.0)
        vh = v_ref[:, vs].astype(BF16)
        o = _dot(att.astype(BF16), vh)
        s = s_scr[h]
        inter = []
        for c in range(R // C):
            rs = slice(c * C, (c + 1) * C)
            inter.append(_dot(qh[rs], s.astype(BF16)))
            kv = _dot(kte_t[ks, rs].astype(BF16), vh[rs])
            s = dec_t[ks, c * C:c * C + 1] * s + kv
        s_scr[h] = s
        o = o + jnp.concatenate(inter, axis=0)
        o = _rms(o, on_ref[...])
        tok_ref[:, vs] = o * _silu(g_ref[:, vs])

    @pl.when(i == pl.num_programs(1) - 1)
    def _():
        s_ref[0] = s_scr[...]


def _gla_prompt(q, k, la, v, g, out_norm, batch, seq):
    R = 256
    nb = seq // R
    rows = lambda b, i: (b * nb + i, 0)
    return pl.pallas_call(
        _gla_prompt_kernel,
        grid=(batch, nb),
        in_specs=[pl.BlockSpec((R, GLA_QK), rows), pl.BlockSpec((R, GLA_QK), rows),
                  pl.BlockSpec((R, GLA_QK), rows), pl.BlockSpec((R, GLA_VW), rows),
                  pl.BlockSpec((R, GLA_VW), rows),
                  pl.BlockSpec((1, GLA_DV), lambda b, i: (0, 0))],
        out_specs=[pl.BlockSpec((R, GLA_VW), rows),
                   pl.BlockSpec((1, GLA_HEADS, GLA_DK, GLA_DV), lambda b, i: (b, 0, 0, 0))],
        out_shape=[jax.ShapeDtypeStruct((batch * seq, GLA_VW), F32),
                   jax.ShapeDtypeStruct((batch, GLA_HEADS, GLA_DK, GLA_DV), F32)],
        scratch_shapes=[pltpu.VMEM((GLA_HEADS, GLA_DK, GLA_DV), F32)],
        compiler_params=_cparams(("parallel", "arbitrary"), 32),
        name="gla_prompt",
    )(q, k, la, v, g, out_norm.reshape(1, GLA_DV))


def _gla_step_kernel(qt_ref, kt_ref, lat_ref, v_ref, g_ref, s0_ref, on_ref, tok_ref, sn_ref):
    nb = v_ref.shape[0]
    for i in range(nb):
        for h in range(GLA_HEADS):
            vs = slice(h * GLA_DV, (h + 1) * GLA_DV)
            qc = qt_ref[i, :, h:h + 1] * (GLA_DK ** -0.5)
            kc = kt_ref[i, :, h:h + 1]
            ac = jnp.exp(lat_ref[i, :, h:h + 1])
            s = ac * s0_ref[i, h] + kc * v_ref[i:i + 1, vs]
            sn_ref[i, h] = s
            o = jnp.sum(qc * s, axis=0, keepdims=True)
            o = _rms(o, on_ref[...])
            tok_ref[i:i + 1, vs] = o * _silu(g_ref[i:i + 1, vs])


def _gla_step(q, k, la, v, g, s0, out_norm):
    nseq = q.shape[0]
    nb = 8
    cols = lambda a: a.reshape(nseq, GLA_HEADS, GLA_DK).transpose(0, 2, 1)
    col_spec = pl.BlockSpec((nb, GLA_DK, GLA_HEADS), lambda i: (i, 0, 0))
    row_spec = pl.BlockSpec((nb, GLA_VW), lambda i: (i, 0))
    st_spec = pl.BlockSpec((nb, GLA_HEADS, GLA_DK, GLA_DV), lambda i: (i, 0, 0, 0))
    return pl.pallas_call(
        _gla_step_kernel,
        grid=(nseq // nb,),
        in_specs=[col_spec, col_spec, col_spec, row_spec, row_spec, st_spec,
                  pl.BlockSpec((1, GLA_DV), lambda i: (0, 0))],
        out_specs=[row_spec, st_spec],
        out_shape=[jax.ShapeDtypeStruct((nseq, GLA_VW), F32),
                   jax.ShapeDtypeStruct(s0.shape, F32)],
        compiler_params=_cparams(("parallel",), 32),
        name="gla_step",
    )(cols(q), cols(k), cols(la), v, g, s0, out_norm.reshape(1, GLA_DV))


def _sb_prompt_kernel(bias_ref, q_ref, kt_ref, v_ref, o_ref, qq_scr, c_scr, acc_scr, *, nq):
    hp = pl.program_id(1)
    i = pl.program_id(2)
    QB = SB_PROMPT_QBLOCK
    lane = lax.broadcasted_iota(jnp.int32, (QB, LANES), 1)
    first = lane < SB_HEAD_DIM
    rowi = lax.broadcasted_iota(jnp.int32, (2 * QB, QB), 0)
    coli = lax.broadcasted_iota(jnp.int32, (2 * QB, QB), 1)
    r1 = lax.broadcasted_iota(jnp.int32, (QB, QB), 0)
    c1 = lax.broadcasted_iota(jnp.int32, (QB, QB), 1)
    later = jnp.where(r1 > c1, 1.0, 0.0).astype(BF16)
    strict = coli < jnp.bitwise_and(rowi, QB - 1)
    b0 = bias_ref[2 * hp]
    b1 = bias_ref[2 * hp + 1]

    def block(kb, qq, c, masked):
        start = pl.multiple_of(kb * QB, QB)
        kk = kt_ref[0, :, pl.ds(start, QB)].astype(BF16)
        vv = v_ref[pl.ds(start, QB), :].astype(BF16)
        z = _dot(qq, kk)
        z = jnp.concatenate([z[:QB] + b0, z[QB:] + b1], axis=0)
        nlf = _softplus(z)
        if masked:
            nlf = jnp.where(strict, nlf, 0.0)
        after = _dot(nlf.astype(BF16), later) + jnp.concatenate([c] * (QB // LANES), axis=1)
        w = jnp.exp((z - nlf) - after)
        if masked:
            w = jnp.where(strict, w, 0.0)
        return c + jnp.sum(nlf, axis=1, keepdims=True), _dot(w.astype(BF16), vv)

    qblocks = (i, nq - 1 - i)
    for sel, qi in enumerate(qblocks):
        q = q_ref[pl.ds(pl.multiple_of(qi * QB, QB), QB), :] * (SB_HEAD_DIM ** -0.5)
        qq = jnp.concatenate([jnp.where(first, q, 0.0), jnp.where(first, 0.0, q)],
                             axis=0).astype(BF16)
        qq_scr[sel] = qq
        c, pv = block(qi, qq, jnp.zeros((2 * QB, LANES), F32), True)
        c_scr[sel] = c
        acc_scr[sel] = pv

    for t in range(nq - 1):
        sel = jnp.where(t < i, 0, 1)
        kb = jnp.where(t < i, i - 1 - t, nq - 2 - t)
        c, pv = block(kb, qq_scr[sel], c_scr[sel], False)
        c_scr[sel] = c
        acc_scr[sel] += pv

    for sel, qi in enumerate(qblocks):
        o_ref[pl.ds(pl.multiple_of(qi * QB, QB), QB), :] = jnp.where(
            first, acc_scr[sel, :QB], acc_scr[sel, QB:])


def _sb_prompt(q, kt, v, bias, batch, seq):
    QB = SB_PROMPT_QBLOCK
    nq = seq // QB
    npair = SB_W // LANES
    rowspec = pl.BlockSpec((seq, LANES), lambda b, hp, i: (b, hp))
    ktspec = pl.BlockSpec((1, LANES, seq), lambda b, hp, i: (b, hp, 0))
    return pl.pallas_call(
        functools.partial(_sb_prompt_kernel, nq=nq),
        grid=(batch, npair, nq // 2),
        in_specs=[pl.BlockSpec(memory_space=pltpu.SMEM), rowspec, ktspec, rowspec],
        out_specs=rowspec,
        out_shape=jax.ShapeDtypeStruct((batch * seq, SB_W), F32),
        scratch_shapes=[pltpu.VMEM((2, 2 * QB, LANES), BF16), pltpu.VMEM((2, 2 * QB, LANES), F32),
                        pltpu.VMEM((2, 2 * QB, LANES), F32)],
        compiler_params=_cparams(("parallel", "parallel", "arbitrary"), 32),
        name="sb_prompt",
    )(bias, q, kt, v)


def _mem_kv_kernel(m_ref, g_ref, w_ref, kn_ref, mk_ref, mv_ref, mkt_ref, mvb_ref):
    h = _rms(m_ref[...], g_ref[...]).astype(BF16)
    kv = _dot(h, w_ref[...])
    for hd in range(MEM_HEADS):
        hs = slice(hd * MEM_HEAD_DIM, (hd + 1) * MEM_HEAD_DIM)
        mk = _rms(kv[:, hs], kn_ref[...])
        mk_ref[:, hs] = mk
        mkt_ref[0, hs, :] = mk.T.astype(BF16)
    mv = kv[:, MEM_WIDTH:]
    mv_ref[...] = mv
    mvb_ref[...] = mv.astype(BF16)


def _mem_kv(mem, gain, w, k_norm, n_mem):
    T, D = mem.shape
    const = lambda i: (0, 0)
    rows = lambda i: (i, 0)
    return pl.pallas_call(
        _mem_kv_kernel,
        grid=(T // n_mem,),
        in_specs=[pl.BlockSpec((n_mem, D), rows), pl.BlockSpec((1, D), const),
                  pl.BlockSpec(w.shape, const), pl.BlockSpec((1, MEM_HEAD_DIM), const)],
        out_specs=[pl.BlockSpec((n_mem, MEM_WIDTH), rows), pl.BlockSpec((n_mem, MEM_WIDTH), rows),
                   pl.BlockSpec((1, MEM_WIDTH, n_mem), lambda i: (i, 0, 0)),
                   pl.BlockSpec((n_mem, MEM_WIDTH), rows)],
        out_shape=[jax.ShapeDtypeStruct((T, MEM_WIDTH), F32), jax.ShapeDtypeStruct((T, MEM_WIDTH), F32),
                   jax.ShapeDtypeStruct((T // n_mem, MEM_WIDTH, n_mem), BF16),
                   jax.ShapeDtypeStruct((T, MEM_WIDTH), BF16)],
        compiler_params=_cparams(("parallel",), 32),
        name="mem_kv",
    )(mem, gain.reshape(1, D), w, k_norm.reshape(1, MEM_HEAD_DIM))


def _mem_attn_step_kernel(q_ref, mk_ref, mv_ref, qn_ref, o_ref):
    for i in range(q_ref.shape[0]):
        q = _rms(q_ref[i], qn_ref[...])
        s = jnp.sum(mk_ref[i] * q, axis=-1, keepdims=True) * (MEM_HEAD_DIM ** -0.5)
        e = jnp.exp(s - jnp.max(s, axis=0, keepdims=True))
        p = e / jnp.sum(e, axis=0, keepdims=True)
        o_ref[i] = jnp.sum(p * mv_ref[i], axis=0)


def _mem_attn_step(mq, cache_k, cache_v, layer, q_norm):
    nseq = mq.shape[0]
    n_mem = cache_k.shape[2]
    nb = 8
    qspec = pl.BlockSpec((nb, MEM_HEADS, MEM_HEAD_DIM), lambda i: (i, 0, 0))
    kvspec = pl.BlockSpec((None, nb, n_mem, MEM_HEADS, MEM_HEAD_DIM), lambda i: (layer, i, 0, 0, 0))
    out = pl.pallas_call(
        _mem_attn_step_kernel,
        grid=(nseq // nb,),
        in_specs=[qspec, kvspec, kvspec, pl.BlockSpec((1, MEM_HEAD_DIM), lambda i: (0, 0))],
        out_specs=qspec,
        out_shape=jax.ShapeDtypeStruct((nseq, MEM_HEADS, MEM_HEAD_DIM), F32),
        compiler_params=_cparams(("parallel",), 32),
        name="mem_attn_step",
    )(mq.reshape(nseq, MEM_HEADS, MEM_HEAD_DIM), cache_k, cache_v, q_norm.reshape(1, MEM_HEAD_DIM))
    return out.reshape(nseq, MEM_WIDTH)


def _out_proj_kernel(x_ref, tok_ref, mo_ref, w_ref, g_ref, x1_ref, h_ref):
    tw = tok_ref.shape[1]
    y = _dot(tok_ref[...].astype(BF16), w_ref[:tw, :]) + _dot(mo_ref[...].astype(BF16), w_ref[tw:, :])
    x1 = x_ref[...] + y
    x1_ref[...] = x1
    h_ref[...] = _rms(x1, g_ref[...]).astype(BF16)


def _out_proj(x, tok, mo, w, gain):
    T, D = x.shape
    tm = min(T, 256)
    const = lambda i: (0, 0)
    rows = lambda i: (i, 0)
    return pl.pallas_call(
        _out_proj_kernel,
        grid=(T // tm,),
        in_specs=[pl.BlockSpec((tm, D), rows), pl.BlockSpec((tm, tok.shape[1]), rows),
                  pl.BlockSpec((tm, mo.shape[1]), rows), pl.BlockSpec(w.shape, const),
                  pl.BlockSpec((1, D), const)],
        out_specs=[pl.BlockSpec((tm, D), rows)] * 2,
        out_shape=[jax.ShapeDtypeStruct((T, D), F32), jax.ShapeDtypeStruct((T, D), BF16)],
        compiler_params=_cparams(("parallel",), 32),
        name="out_proj",
    )(x, tok, mo, w, gain.reshape(1, D))


def _tail_prompt_kernel(pt_ref, bias_ref, x_ref, tok_ref, mq_ref, mkt_ref, mv_ref, qn_ref, wo_ref,
                        g_ref, wu_ref, cw_ref, cb_ref, wd_ref, qb_ref, kt_hbm, vt_hbm,
                        o_ref, cs_ref, dec_ref,
                        carry_scr, h_scr, act_scr, kbuf, vbuf, sem, dacc_scr, drun_scr,
                        *, seq_base, steps_per_seq, sb_layer):
    i = pl.program_id(1)
    nb = pl.num_programs(1)
    step = pl.program_id(0) * nb + i
    nsteps = pl.num_programs(0) * nb
    CH = DECODE_CHUNK
    H = SB_HEADS
    P = kbuf.shape[-1]
    chunks_per_seq = pt_ref.shape[1] // CH
    cps = chunks_per_seq // steps_per_seq
    seq = seq_base + step // steps_per_seq
    part = step % steps_per_seq

    def chunk_copies(sq, c, slot):
        copies = []
        for p in range(CH):
            page = pt_ref[sq, (chunks_per_seq - 1 - c) * CH + p]
            copies.append(pltpu.make_async_copy(kt_hbm.at[sb_layer, page], kbuf.at[slot, p],
                                                sem.at[slot]))
            copies.append(pltpu.make_async_copy(vt_hbm.at[sb_layer, page], vbuf.at[slot, p],
                                                sem.at[slot]))
        return copies

    @pl.when(step == 0)
    def _():
        dacc_scr[...] = jnp.zeros_like(dacc_scr)
        drun_scr[...] = jnp.zeros_like(drun_scr)
        for cp in chunk_copies(seq, 0, 0):
            cp.start()

    @pl.when(i == 0)
    def _():
        carry_scr[...] = jnp.zeros_like(carry_scr)

    sub = lax.broadcasted_iota(jnp.int32, (H, P), 0)
    dbias = jnp.zeros((H, P), F32)
    for hh in range(H):
        dbias = jnp.where(sub == hh, bias_ref[hh], dbias)
    prow = lax.broadcasted_iota(jnp.int32, (P, P), 0)
    pcol = lax.broadcasted_iota(jnp.int32, (P, P), 1)
    later = jnp.where(prow > pcol, 1.0, 0.0).astype(BF16)

    def decode_chunk(j):
        c = part * cps + j
        slot = j % 2
        if j + 1 < cps:
            for cp in chunk_copies(seq, c + 1, 1 - slot):
                cp.start()
        else:
            @pl.when(step + 1 < nsteps)
            def _():
                for cp in chunk_copies(seq + (c + 1) // chunks_per_seq, (c + 1) % chunks_per_seq,
                                       1 - slot):
                    cp.start()
        for cp in chunk_copies(seq, c, slot):
            cp.wait()
        zs = []
        for p in range(CH):
            zp = dbias
            for hh in range(H):
                zh = jnp.sum(kbuf[slot, p, hh] * qb_ref[0, hh], axis=0, keepdims=True)
                zp = jnp.where(sub == hh, zh * (SB_HEAD_DIM ** -0.5) + dbias, zp)
            zs.append(zp)
        z = jnp.concatenate(zs, axis=0)
        nlf = _softplus(z)
        after = _dot(nlf.astype(BF16), later)
        run = drun_scr[...]
        ws = [None] * CH
        for p in reversed(range(CH)):
            rs = slice(p * H, (p + 1) * H)
            ws[p] = jnp.exp((z[rs] - nlf[rs]) - (after[rs] + run))
            run = run + jnp.sum(nlf[rs], axis=1, keepdims=True)
        drun_scr[...] = run
        for hh in range(H):
            a = dacc_scr[hh]
            for p in range(CH):
                a = a + vbuf[slot, p, hh] * ws[p][hh:hh + 1, :]
            dacc_scr[hh] = a

    tm = x_ref.shape[0]
    tw = tok_ref.shape[1]
    y = _dot(tok_ref[...].astype(BF16), wo_ref[:tw, :])
    for hd in range(MEM_HEADS):
        hs = slice(hd * MEM_HEAD_DIM, (hd + 1) * MEM_HEAD_DIM)
        qh = _rms(mq_ref[:, hs], qn_ref[...]).astype(BF16)
        sc = _dot(qh, mkt_ref[0, hs, :]) * (MEM_HEAD_DIM ** -0.5)
        e = jnp.exp(sc - jnp.max(sc, axis=-1, keepdims=True))
        o = _dot(e.astype(BF16), mv_ref[:, hs]) / jnp.sum(e, axis=-1, keepdims=True)
        y = y + _dot(o.astype(BF16), wo_ref[tw + hd * MEM_HEAD_DIM:tw + (hd + 1) * MEM_HEAD_DIM, :])
    x1 = x_ref[...] + y
    o_ref[...] = x1
    h_scr[...] = _rms(x1, g_ref[...]).astype(BF16)

    FC = FFN_CHUNK
    h = h_scr[...]
    rowi = lax.broadcasted_iota(jnp.int32, (tm, FC), 0)
    for c in range(D_FF // FC):
        cs = slice(c * FC, (c + 1) * FC)
        gate = _dot(h, wu_ref[:, cs])
        val = _dot(h, wu_ref[:, D_FF + c * FC:D_FF + (c + 1) * FC])
        p0 = carry_scr[0:1, cs]
        p1 = carry_scr[1:2, cs]
        g1 = jnp.where(rowi == 0, p1, pltpu.roll(gate, 1, 0))
        g2 = jnp.where(rowi == 0, p0, jnp.where(rowi == 1, p1, pltpu.roll(gate, 2, 0)))
        a = cb_ref[:, cs] + g2 * cw_ref[0:1, cs] + g1 * cw_ref[1:2, cs] + gate * cw_ref[2:3, cs]
        act_scr[:, cs] = (_silu(a) * val).astype(BF16)
        last = gate[tm - (FFN_CONV - 1):, :]
        carry_scr[0:FFN_CONV - 1, cs] = last
        cs_ref[0, :, cs] = last
        if c < cps:
            decode_chunk(c)
    o_ref[...] += _dot(act_scr[...], wd_ref[...])

    @pl.when(part == steps_per_seq - 1)
    def _():
        ones = jnp.ones((H, P), BF16)
        subd = lax.broadcasted_iota(jnp.int32, (H, SB_HEAD_DIM), 0)
        out = jnp.zeros((H, SB_HEAD_DIM), F32)
        for hh in range(H):
            hi, lo = _split_bf16(dacc_scr[hh])
            out = jnp.where(subd == hh, _dot_nt(ones, hi) + _dot_nt(ones, lo), out)
        dec_ref[0] = out
        dacc_scr[...] = jnp.zeros_like(dacc_scr)
        drun_scr[...] = jnp.zeros_like(drun_scr)


def _tail_prompt(x, tok, mq, mkt, mvb, q_norm, w_o, gain, w_up, conv_w, conv_b, w_down, layer,
                 batch, seq, dec_q, cache_kt, cache_vt, sb_layer, page_table, sb_bias, seq_base,
                 n_dec):
    T, D = x.shape
    n_mem = mkt.shape[2]
    tm = 512
    nb = seq // tm
    steps_per_seq = (batch * nb) // n_dec
    n_pages = page_table.shape[1]
    P = cache_kt.shape[-1]
    CH = DECODE_CHUNK
    assert steps_per_seq * n_dec == batch * nb
    assert n_pages % (CH * steps_per_seq) == 0 and (n_pages // (CH * steps_per_seq)) % 2 == 0
    assert n_pages // (CH * steps_per_seq) <= D_FF // FFN_CHUNK
    this_layer = lambda b, i, pt: (layer, 0, 0)
    rows = lambda b, i, pt: (b * nb + i, 0)
    const = lambda b, i, pt: (0, 0)
    once = pl.Buffered(1)
    grid_spec = pltpu.PrefetchScalarGridSpec(
        num_scalar_prefetch=1,
        grid=(batch, nb),
        in_specs=[pl.BlockSpec(memory_space=pltpu.SMEM),
                  pl.BlockSpec((tm, D), rows), pl.BlockSpec((tm, tok.shape[1]), rows),
                  pl.BlockSpec((tm, MEM_WIDTH), rows),
                  pl.BlockSpec((1, MEM_WIDTH, n_mem), lambda b, i, pt: (b, 0, 0)),
                  pl.BlockSpec((n_mem, MEM_WIDTH), lambda b, i, pt: (b, 0)),
                  pl.BlockSpec((1, MEM_HEAD_DIM), const),
                  pl.BlockSpec(w_o.shape, const, pipeline_mode=once),
                  pl.BlockSpec((1, D), const),
                  pl.BlockSpec((None,) + w_up.shape[1:], this_layer, pipeline_mode=once),
                  pl.BlockSpec(conv_w.shape, const), pl.BlockSpec((1, D_FF), const),
                  pl.BlockSpec((None,) + w_down.shape[1:], this_layer, pipeline_mode=once),
                  pl.BlockSpec((1, SB_HEADS, SB_HEAD_DIM, P),
                               lambda b, i, pt: (seq_base + (b * nb + i) // steps_per_seq, 0, 0, 0)),
                  pl.BlockSpec(memory_space=pl.ANY), pl.BlockSpec(memory_space=pl.ANY)],
        out_specs=[pl.BlockSpec((tm, D), rows),
                   pl.BlockSpec((1, FFN_CONV - 1, D_FF), lambda b, i, pt: (b, 0, 0)),
                   pl.BlockSpec((1, SB_HEADS, SB_HEAD_DIM),
                                lambda b, i, pt: ((b * nb + i) // steps_per_seq, 0, 0))],
        scratch_shapes=[pltpu.VMEM((8, D_FF), F32), pltpu.VMEM((tm, D), BF16),
                        pltpu.VMEM((tm, D_FF), BF16),
                        pltpu.VMEM((2, CH, SB_HEADS, SB_HEAD_DIM, P), F32),
                        pltpu.VMEM((2, CH, SB_HEADS, SB_HEAD_DIM, P), F32),
                        pltpu.SemaphoreType.DMA((2,)),
                        pltpu.VMEM((SB_HEADS, SB_HEAD_DIM, P), F32),
                        pltpu.VMEM((SB_HEADS, P), F32)],
    )
    return pl.pallas_call(
        functools.partial(_tail_prompt_kernel, seq_base=seq_base, steps_per_seq=steps_per_seq,
                          sb_layer=sb_layer),
        grid_spec=grid_spec,
        out_shape=[jax.ShapeDtypeStruct((T, D), F32),
                   jax.ShapeDtypeStruct((batch, FFN_CONV - 1, D_FF), F32),
                   jax.ShapeDtypeStruct((n_dec, SB_HEADS, SB_HEAD_DIM), F32)],
        compiler_params=_cparams(("arbitrary", "arbitrary"), 60),
        name="tail_prompt",
    )(page_table, sb_bias, x, tok, mq, mkt, mvb, q_norm.reshape(1, MEM_HEAD_DIM), w_o,
      gain.reshape(1, D), w_up, conv_w, conv_b.reshape(1, D_FF), w_down, dec_q, cache_kt, cache_vt)


def _ffn_step_kernel(h_ref, x_ref, wg_ref, wv_ref, cw_ref, cb_ref, wd_ref, p0_ref, p1_ref,
                     o_ref, gate_ref):
    @pl.when(pl.program_id(0) == 0)
    def _():
        o_ref[...] = x_ref[...]

    h = h_ref[...]
    gate = _dot(h, wg_ref[...])
    val = _dot(h, wv_ref[...])
    a = (cb_ref[...] + p0_ref[...] * cw_ref[0:1, :] + p1_ref[...] * cw_ref[1:2, :]
         + gate * cw_ref[2:3, :])
    o_ref[...] += _dot((_silu(a) * val).astype(BF16), wd_ref[...])
    gate_ref[...] = gate


def _ffn_step(h, x1, w_up, conv_w, conv_b, w_down, layer, prev):
    T, D = x1.shape
    FC = FFN_CHUNK
    nch = D_FF // FC
    const = lambda c: (0, 0)
    chunk = lambda c: (0, c)
    out, gate = pl.pallas_call(
        _ffn_step_kernel,
        grid=(nch,),
        in_specs=[pl.BlockSpec((T, D), const), pl.BlockSpec((T, D), const),
                  pl.BlockSpec((None, D, FC), lambda c: (layer, 0, c)),
                  pl.BlockSpec((None, D, FC), lambda c: (layer, 0, nch + c)),
                  pl.BlockSpec((FFN_CONV, FC), chunk), pl.BlockSpec((1, FC), chunk),
                  pl.BlockSpec((None, FC, D), lambda c: (layer, c, 0)),
                  pl.BlockSpec((T, FC), chunk), pl.BlockSpec((T, FC), chunk)],
        out_specs=[pl.BlockSpec((T, D), const), pl.BlockSpec((T, FC), chunk)],
        out_shape=[jax.ShapeDtypeStruct((T, D), F32), jax.ShapeDtypeStruct((T, D_FF), F32)],
        compiler_params=_cparams(("arbitrary",), 32),
        name="ffn_step",
    )(h, x1, w_up, w_up, conv_w, conv_b.reshape(1, D_FF), w_down, prev[:, 0], prev[:, 1])
    return out, jnp.stack([prev[:, 1], gate], axis=1)


def kernel(x_prompt, x_sample, state_gla, cache_sb_k, cache_sb_v, cache_mem_k, cache_mem_v, state_ffn_conv, page_table, mem_prompt, norm_mix, w_in_gla, w_gate_up, b_gate, gla_out_norm, w_in_sb, sb_bias, mem_norm, w_mem_kv, mem_q_norm, mem_k_norm, w_out, norm_ffn, w_ffn_up, ffn_conv_w, ffn_conv_b, w_ffn_down):
    batch, seq, d_model = x_prompt.shape
    nseq = x_sample.shape[0]
    n_mem = mem_prompt.shape[1]
    depth = norm_mix.shape[0]
    gd0 = 2 * GLA_QK + 2 * GLA_VW
    gla_widths = (GLA_QK, GLA_QK, GLA_VW, GLA_VW, MEM_WIDTH)
    sb_widths = (SB_W, SB_W, SB_W, MEM_WIDTH)

    def gla_weights(j):
        w = w_in_gla[j]
        w_main = jnp.concatenate([w[:, :gd0], w[:, gd0 + GLA_GATE_RANK:]], axis=1).astype(BF16)
        w_gd = jnp.pad(w[:, gd0:gd0 + GLA_GATE_RANK], ((0, 0), (0, LANES - GLA_GATE_RANK))).astype(BF16)
        w_gu = jnp.pad(w_gate_up[j], ((0, LANES - GLA_GATE_RANK), (0, 0))).astype(BF16)
        return w_main, (w_gd, w_gu, b_gate[j])

    per_layer_bf16 = lambda w: [w[i].astype(BF16) for i in range(w.shape[0])]
    w_out_b = per_layer_bf16(w_out)
    w_up_b = w_ffn_up.astype(BF16)
    w_down_b = w_ffn_down.astype(BF16)
    w_kv_b = per_layer_bf16(w_mem_kv)
    w_sb_b = per_layer_bf16(w_in_sb)
    gla_w = [gla_weights(j) for j in range(w_in_gla.shape[0])]

    assert depth == 2 and x_sample.shape[1] == 1 and cache_sb_k.shape[0] == 1
    dec_seq = x_sample.shape[1]
    P = cache_sb_k.shape[2]

    def sample_front(x, i):
        j = i // 2
        if i % 2 == 0:
            w_main, gate = gla_w[j]
            q, k, v, g, mq, la = _proj(x, norm_mix[i], w_main, gla_widths, gate)
            tok, s_new = _gla_step(q, k, la, v, g, state_gla[j], gla_out_norm[j])
            return tok, mq, s_new
        q, k, v, mq = _proj(x, norm_mix[i], w_sb_b[j], sb_widths)
        return q, mq, (k, v)

    def sample_back(x, tok, mq, i):
        mo = _mem_attn_step(mq, cache_mem_k, cache_mem_v, i, mem_q_norm[i])
        x1, h2 = _out_proj(x, tok, mo, w_out_b[i], norm_ffn[i])
        return _ffn_step(h2, x1, w_up_b, ffn_conv_w[i], ffn_conv_b[i], w_down_b, i,
                         state_ffn_conv[i])

    xs = x_sample.reshape(nseq * dec_seq, d_model)
    tok0, mq0, s_gla0 = sample_front(xs, 0)
    xs, conv_s0 = sample_back(xs, tok0, mq0, 0)
    q_sb, mq1, (k_sb, v_sb) = sample_front(xs, 1)
    cache_kt = jnp.transpose(cache_sb_k, (0, 1, 3, 4, 2))
    cache_vt = jnp.transpose(cache_sb_v, (0, 1, 3, 4, 2))
    dec_q = jnp.broadcast_to(q_sb.reshape(nseq, SB_HEADS, SB_HEAD_DIM, 1),
                             (nseq, SB_HEADS, SB_HEAD_DIM, P))
    n_dec = nseq // depth
    dec_out = []

    x = x_prompt.reshape(batch * seq, d_model)
    mem = mem_prompt.reshape(batch * n_mem, d_model)
    p_gla, p_sbk, p_sbv, p_mk, p_mv, p_conv = [], [], [], [], [], []
    for i in range(depth):
        j = i // 2
        if i % 2 == 0:
            w_main, gate = gla_w[j]
            q, k, v, g, mq, la = _proj(x, norm_mix[i], w_main, gla_widths, gate)
            tok, s_fin = _gla_prompt(q, k, la, v, g, gla_out_norm[j], batch, seq)
            p_gla.append(s_fin)
        else:
            q, kt, v, vt, mq = _proj_sb_prompt(x, norm_mix[i], w_sb_b[j], batch, seq)
            tok = _sb_prompt(q, kt, v, sb_bias[j], batch, seq)
            by_pos = lambda t: t.reshape(batch, SB_HEADS, SB_HEAD_DIM, seq).transpose(0, 3, 1, 2)
            p_sbk.append(by_pos(kt))
            p_sbv.append(by_pos(vt))
        mk, mv, mkt, mvb = _mem_kv(mem, mem_norm[i], w_kv_b[i], mem_k_norm[i], n_mem)
        p_mk.append(mk.reshape(batch, n_mem, MEM_HEADS, MEM_HEAD_DIM))
        p_mv.append(mv.reshape(batch, n_mem, MEM_HEADS, MEM_HEAD_DIM))
        x, conv, dec = _tail_prompt(x, tok, mq, mkt, mvb, mem_q_norm[i], w_out_b[i], norm_ffn[i],
                                    w_up_b, ffn_conv_w[i], ffn_conv_b[i], w_down_b, i, batch, seq,
                                    dec_q, cache_kt, cache_vt, 0, page_table, sb_bias[0],
                                    i * n_dec, n_dec)
        p_conv.append(conv)
        dec_out.append(dec)
    y_prompt = x.reshape(batch, seq, d_model)

    tok1 = jnp.concatenate(dec_out, axis=0).reshape(nseq, SB_W)
    xs, conv_s1 = sample_back(xs, tok1, mq1, 1)
    y_sample = xs.reshape(nseq, dec_seq, d_model)
    s_gla = [s_gla0]
    s_sbk = [k_sb.reshape(nseq, dec_seq, SB_HEADS, SB_HEAD_DIM)]
    s_sbv = [v_sb.reshape(nseq, dec_seq, SB_HEADS, SB_HEAD_DIM)]
    s_conv = [conv_s0, conv_s1]

    return (y_prompt, y_sample,
            jnp.stack(p_gla), jnp.stack(s_gla),
            jnp.stack(p_sbk), jnp.stack(p_sbv), jnp.stack(s_sbk), jnp.stack(s_sbv),
            jnp.stack(p_mk), jnp.stack(p_mv),
            jnp.stack(p_conv), jnp.stack(s_conv))
```

```python
import functools

import jax
import jax.numpy as jnp
from jax import lax
from jax.experimental import pallas as pl
from jax.experimental.pallas import tpu as pltpu

F32 = jnp.float32
BF16 = jnp.bfloat16

GLA_HEADS = 4
GLA_DK = 64
GLA_DV = 128
GLA_GATE_RANK = 16
GLA_TAU = 16.0
GLA_CHUNK = 64
SB_HEADS = 8
SB_HEAD_DIM = 64
SB_QBLOCK = 128
MEM_HEADS = 4
MEM_HEAD_DIM = 128
D_FF = 2816
FFN_CONV = 3
EPS = 1e-6

GLA_QK = GLA_HEADS * GLA_DK
GLA_VW = GLA_HEADS * GLA_DV
SB_W = SB_HEADS * SB_HEAD_DIM
MEM_WIDTH = MEM_HEADS * MEM_HEAD_DIM

LOG2E = 1.4426950408889634
LANES = 128
FFN_CHUNK = 256
SB_PROMPT_QBLOCK = 2 * SB_QBLOCK
DECODE_CHUNK = 8
DECODE_SLOTS = 4


def _cparams(semantics, vmem_mib):
    return pltpu.CompilerParams(dimension_semantics=semantics,
                                vmem_limit_bytes=vmem_mib << 20)


def _dot(a, b):
    return jnp.dot(a, b, preferred_element_type=F32)


def _dot_nt(a, b):
    return lax.dot_general(a, b, (((1,), (1,)), ((), ())), preferred_element_type=F32)


def _rms(xf, g):
    ms = jnp.mean(xf * xf, axis=-1, keepdims=True)
    return xf * lax.rsqrt(ms + EPS) * g


def _log_sigmoid_pair(z):
    t = jnp.log1p(jnp.exp(-jnp.abs(z)))
    return jnp.minimum(z, 0.0) - t, jnp.minimum(-z, 0.0) - t


def _softplus(z):
    return jnp.maximum(z, 0.0) + jnp.log(1.0 + jnp.exp2(jnp.abs(z) * (-LOG2E)))


def _silu(x):
    return x * jax.nn.sigmoid(x)


def _split_bf16(x):
    hi = x.astype(BF16)
    lo = (x - hi.astype(F32)).astype(BF16)
    return hi, lo


def _dot_split(x, m):
    hi, lo = _split_bf16(x)
    return _dot(hi, m) + _dot(lo, m)


def _dot_split_left(m, x):
    hi, lo = _split_bf16(x)
    return _dot(m, hi) + _dot(m, lo)


def _proj_kernel(*refs, widths, gla):
    if gla:
        x_ref, g_ref, w_ref, wgd_ref, wgu_ref, bg_ref = refs[:6]
        outs = refs[6:]
    else:
        x_ref, g_ref, w_ref = refs[:3]
        outs = refs[3:]
    h = _rms(x_ref[...], g_ref[...]).astype(BF16)
    z = _dot(h, w_ref[...])
    off = 0
    for o_ref, wd in zip(outs, widths):
        o_ref[...] = z[:, off:off + wd]
        off += wd
    if gla:
        gd = _dot(h, wgd_ref[...])
        pre = _dot(gd.astype(BF16), wgu_ref[...]) + bg_ref[...]
        ls, _ = _log_sigmoid_pair(pre)
        outs[-1][...] = ls * (1.0 / GLA_TAU)


def _proj(x, gain, w, widths, gate=None):
    T, D = x.shape
    tm = min(T, 512)
    const = lambda i: (0, 0)
    in_specs = [pl.BlockSpec((tm, D), lambda i: (i, 0)),
                pl.BlockSpec((1, D), const),
                pl.BlockSpec(w.shape, const)]
    args = [x, gain.reshape(1, D), w]
    out_widths = list(widths)
    if gate is not None:
        wgd, wgu, bg = gate
        in_specs += [pl.BlockSpec(wgd.shape, const), pl.BlockSpec(wgu.shape, const),
                     pl.BlockSpec((1, GLA_QK), const)]
        args += [wgd, wgu, bg.reshape(1, GLA_QK)]
        out_widths.append(GLA_QK)
    return pl.pallas_call(
        functools.partial(_proj_kernel, widths=tuple(widths), gla=gate is not None),
        grid=(T // tm,),
        in_specs=in_specs,
        out_specs=[pl.BlockSpec((tm, wd), lambda i: (i, 0)) for wd in out_widths],
        out_shape=[jax.ShapeDtypeStruct((T, wd), F32) for wd in out_widths],
        compiler_params=_cparams(("parallel",), 40),
        name="proj_gla" if gate is not None else "proj_sb",
    )(*args)


def _proj_sb_prompt_kernel(x_ref, g_ref, w_ref, q_ref, kt_ref, v_ref, vt_ref, mq_ref):
    h = _rms(x_ref[...], g_ref[...]).astype(BF16)
    z = _dot(h, w_ref[...])
    q_ref[...] = z[:, :SB_W]
    k = z[:, SB_W:2 * SB_W]
    v = z[:, 2 * SB_W:3 * SB_W]
    kt_ref[0] = k.T
    v_ref[...] = v
    vt_ref[0] = v.T
    mq_ref[...] = z[:, 3 * SB_W:]


def _proj_sb_prompt(x, gain, w, batch, seq):
    T, D = x.shape
    tm = 512
    nb = seq // tm
    const = lambda i: (0, 0)
    rows = lambda i: (i, 0)
    cols = lambda i: (i // nb, 0, i % nb)
    row_spec = pl.BlockSpec((tm, SB_W), rows)
    col_spec = pl.BlockSpec((1, SB_W, tm), cols)
    row_shape = jax.ShapeDtypeStruct((T, SB_W), F32)
    col_shape = jax.ShapeDtypeStruct((batch, SB_W, seq), F32)
    return pl.pallas_call(
        _proj_sb_prompt_kernel,
        grid=(T // tm,),
        in_specs=[pl.BlockSpec((tm, D), rows), pl.BlockSpec((1, D), const),
                  pl.BlockSpec(w.shape, const)],
        out_specs=[row_spec, col_spec, row_spec, col_spec, row_spec],
        out_shape=[row_shape, col_shape, row_shape, col_shape, row_shape],
        compiler_params=_cparams(("parallel",), 48),
        name="proj_sb_prompt",
    )(x, gain.reshape(1, D), w)


def _gla_prompt_kernel(q_ref, k_ref, la_ref, v_ref, g_ref, on_ref, tok_ref, s_ref, s_scr):
    i = pl.program_id(1)

    @pl.when(i == 0)
    def _():
        s_scr[...] = jnp.zeros_like(s_scr)

    R = q_ref.shape[0]
    C = GLA_CHUNK
    row = lax.broadcasted_iota(jnp.int32, (R, R), 0)
    col = lax.broadcasted_iota(jnp.int32, (R, R), 1)
    same = (row // C) == (col // C)
    causal = jnp.logical_and(same, col <= row)
    ltri = jnp.where(causal, 1.0, 0.0).astype(BF16)
    ones_bd = jnp.where(same, 1.0, 0.0).astype(BF16)

    la = la_ref[...]
    b = _dot_split_left(ltri, la)
    bt = _dot_split_left(ones_bd, la)
    q_dec = q_ref[...] * (GLA_DK ** -0.5) * jnp.exp(b)
    k = k_ref[...]
    k_dec = k * jnp.exp(-b)
    kte_t = (k * jnp.exp(bt - b)).T
    dec_t = jnp.exp(bt).T

    for h in range(GLA_HEADS):
        ks = slice(h * GLA_DK, (h + 1) * GLA_DK)
        vs = slice(h * GLA_DV, (h + 1) * GLA_DV)
        qh = q_dec[:, ks].astype(BF16)
        kh = k_dec[:, ks].astype(BF16)
        att = jnp.where(causal, _dot_nt(qh, kh), 0.0)
        vh = v_ref[:, vs].astype(BF16)
        o = _dot(att.astype(BF16), vh)
        s = s_scr[h]
        inter = []
        for c in range(R // C):
            rs = slice(c * C, (c + 1) * C)
            inter.append(_dot(qh[rs], s.astype(BF16)))
            kv = _dot(kte_t[ks, rs].astype(BF16), vh[rs])
            s = dec_t[ks, c * C:c * C + 1] * s + kv
        s_scr[h] = s
        o = o + jnp.concatenate(inter, axis=0)
        o = _rms(o, on_ref[...])
        tok_ref[:, vs] = o * _silu(g_ref[:, vs])

    @pl.when(i == pl.num_programs(1) - 1)
    def _():
        s_ref[0] = s_scr[...]


def _gla_prompt(q, k, la, v, g, out_norm, batch, seq):
    R = 256
    nb = seq // R
    rows = lambda b, i: (b * nb + i, 0)
    return pl.pallas_call(
        _gla_prompt_kernel,
        grid=(batch, nb),
        in_specs=[pl.BlockSpec((R, GLA_QK), rows), pl.BlockSpec((R, GLA_QK), rows),
                  pl.BlockSpec((R, GLA_QK), rows), pl.BlockSpec((R, GLA_VW), rows),
                  pl.BlockSpec((R, GLA_VW), rows),
                  pl.BlockSpec((1, GLA_DV), lambda b, i: (0, 0))],
        out_specs=[pl.BlockSpec((R, GLA_VW), rows),
                   pl.BlockSpec((1, GLA_HEADS, GLA_DK, GLA_DV), lambda b, i: (b, 0, 0, 0))],
        out_shape=[jax.ShapeDtypeStruct((batch * seq, GLA_VW), F32),
                   jax.ShapeDtypeStruct((batch, GLA_HEADS, GLA_DK, GLA_DV), F32)],
        scratch_shapes=[pltpu.VMEM((GLA_HEADS, GLA_DK, GLA_DV), F32)],
        compiler_params=_cparams(("parallel", "arbitrary"), 32),
        name="gla_prompt",
    )(q, k, la, v, g, out_norm.reshape(1, GLA_DV))


def _gla_step_kernel(qt_ref, kt_ref, lat_ref, v_ref, g_ref, s0_ref, on_ref, tok_ref, sn_ref):
    nb = v_ref.shape[0]
    for i in range(nb):
        for h in range(GLA_HEADS):
            vs = slice(h * GLA_DV, (h + 1) * GLA_DV)
            qc = qt_ref[i, :, h:h + 1] * (GLA_DK ** -0.5)
            kc = kt_ref[i, :, h:h + 1]
            ac = jnp.exp(lat_ref[i, :, h:h + 1])
            s = ac * s0_ref[i, h] + kc * v_ref[i:i + 1, vs]
            sn_ref[i, h] = s
            o = jnp.sum(qc * s, axis=0, keepdims=True)
            o = _rms(o, on_ref[...])
            tok_ref[i:i + 1, vs] = o * _silu(g_ref[i:i + 1, vs])


def _gla_step(q, k, la, v, g, s0, out_norm):
    nseq = q.shape[0]
    nb = 8
    cols = lambda a: a.reshape(nseq, GLA_HEADS, GLA_DK).transpose(0, 2, 1)
    col_spec = pl.BlockSpec((nb, GLA_DK, GLA_HEADS), lambda i: (i, 0, 0))
    row_spec = pl.BlockSpec((nb, GLA_VW), lambda i: (i, 0))
    st_spec = pl.BlockSpec((nb, GLA_HEADS, GLA_DK, GLA_DV), lambda i: (i, 0, 0, 0))
    return pl.pallas_call(
        _gla_step_kernel,
        grid=(nseq // nb,),
        in_specs=[col_spec, col_spec, col_spec, row_spec, row_spec, st_spec,
                  pl.BlockSpec((1, GLA_DV), lambda i: (0, 0))],
        out_specs=[row_spec, st_spec],
        out_shape=[jax.ShapeDtypeStruct((nseq, GLA_VW), F32),
                   jax.ShapeDtypeStruct(s0.shape, F32)],
        compiler_params=_cparams(("parallel",), 32),
        name="gla_step",
    )(cols(q), cols(k), cols(la), v, g, s0, out_norm.reshape(1, GLA_DV))


def _sb_prompt_kernel(bias_ref, q_ref, kt_ref, v_ref, o_ref, qq_scr, c_scr, acc_scr, *, nq):
    hp = pl.program_id(1)
    i = pl.program_id(2)
    QB = SB_PROMPT_QBLOCK
    lane = lax.broadcasted_iota(jnp.int32, (QB, LANES), 1)
    first = lane < SB_HEAD_DIM
    rowi = lax.broadcasted_iota(jnp.int32, (2 * QB, QB), 0)
    coli = lax.broadcasted_iota(jnp.int32, (2 * QB, QB), 1)
    r1 = lax.broadcasted_iota(jnp.int32, (QB, QB), 0)
    c1 = lax.broadcasted_iota(jnp.int32, (QB, QB), 1)
    later = jnp.where(r1 > c1, 1.0, 0.0).astype(BF16)
    strict = coli < jnp.bitwise_and(rowi, QB - 1)
    b0 = bias_ref[2 * hp]
    b1 = bias_ref[2 * hp + 1]

    def block(kb, qq, c, masked):
        start = pl.multiple_of(kb * QB, QB)
        kk = kt_ref[0, :, pl.ds(start, QB)].astype(BF16)
        vv = v_ref[pl.ds(start, QB), :].astype(BF16)
        z = _dot(qq, kk)
        z = jnp.concatenate([z[:QB] + b0, z[QB:] + b1], axis=0)
        nlf = _softplus(z)
        if masked:
            nlf = jnp.where(strict, nlf, 0.0)
        after = _dot(nlf.astype(BF16), later) + jnp.concatenate([c] * (QB // LANES), axis=1)
        w = jnp.exp((z - nlf) - after)
        if masked:
            w = jnp.where(strict, w, 0.0)
        return c + jnp.sum(nlf, axis=1, keepdims=True), _dot(w.astype(BF16), vv)

    qblocks = (i, nq - 1 - i)
    for sel, qi in enumerate(qblocks):
        q = q_ref[pl.ds(pl.multiple_of(qi * QB, QB), QB), :] * (SB_HEAD_DIM ** -0.5)
        qq = jnp.concatenate([jnp.where(first, q, 0.0), jnp.where(first, 0.0, q)],
                             axis=0).astype(BF16)
        qq_scr[sel] = qq
        c, pv = block(qi, qq, jnp.zeros((2 * QB, LANES), F32), True)
        c_scr[sel] = c
        acc_scr[sel] = pv

    for t in range(nq - 1):
        sel = jnp.where(t < i, 0, 1)
        kb = jnp.where(t < i, i - 1 - t, nq - 2 - t)
        c, pv = block(kb, qq_scr[sel], c_scr[sel], False)
        c_scr[sel] = c
        acc_scr[sel] += pv

    for sel, qi in enumerate(qblocks):
        o_ref[pl.ds(pl.multiple_of(qi * QB, QB), QB), :] = jnp.where(
            first, acc_scr[sel, :QB], acc_scr[sel, QB:])


def _sb_prompt(q, kt, v, bias, batch, seq):
    QB = SB_PROMPT_QBLOCK
    nq = seq // QB
    npair = SB_W // LANES
    rowspec = pl.BlockSpec((seq, LANES), lambda b, hp, i: (b, hp))
    ktspec = pl.BlockSpec((1, LANES, seq), lambda b, hp, i: (b, hp, 0))
    return pl.pallas_call(
        functools.partial(_sb_prompt_kernel, nq=nq),
        grid=(batch, npair, nq // 2),
        in_specs=[pl.BlockSpec(memory_space=pltpu.SMEM), rowspec, ktspec, rowspec],
        out_specs=rowspec,
        out_shape=jax.ShapeDtypeStruct((batch * seq, SB_W), F32),
        scratch_shapes=[pltpu.VMEM((2, 2 * QB, LANES), BF16), pltpu.VMEM((2, 2 * QB, LANES), F32),
                        pltpu.VMEM((2, 2 * QB, LANES), F32)],
        compiler_params=_cparams(("parallel", "parallel", "arbitrary"), 32),
        name="sb_prompt",
    )(bias, q, kt, v)


def _mem_kv_kernel(m_ref, g_ref, w_ref, kn_ref, mk_ref, mv_ref, mkt_ref, mvb_ref):
    h = _rms(m_ref[...], g_ref[...]).astype(BF16)
    kv = _dot(h, w_ref[...])
    for hd in range(MEM_HEADS):
        hs = slice(hd * MEM_HEAD_DIM, (hd + 1) * MEM_HEAD_DIM)
        mk = _rms(kv[:, hs], kn_ref[...])
        mk_ref[:, hs] = mk
        mkt_ref[0, hs, :] = mk.T.astype(BF16)
    mv = kv[:, MEM_WIDTH:]
    mv_ref[...] = mv
    mvb_ref[...] = mv.astype(BF16)


def _mem_kv(mem, gain, w, k_norm, n_mem):
    T, D = mem.shape
    const = lambda i: (0, 0)
    rows = lambda i: (i, 0)
    return pl.pallas_call(
        _mem_kv_kernel,
        grid=(T // n_mem,),
        in_specs=[pl.BlockSpec((n_mem, D), rows), pl.BlockSpec((1, D), const),
                  pl.BlockSpec(w.shape, const), pl.BlockSpec((1, MEM_HEAD_DIM), const)],
        out_specs=[pl.BlockSpec((n_mem, MEM_WIDTH), rows), pl.BlockSpec((n_mem, MEM_WIDTH), rows),
                   pl.BlockSpec((1, MEM_WIDTH, n_mem), lambda i: (i, 0, 0)),
                   pl.BlockSpec((n_mem, MEM_WIDTH), rows)],
        out_shape=[jax.ShapeDtypeStruct((T, MEM_WIDTH), F32), jax.ShapeDtypeStruct((T, MEM_WIDTH), F32),
                   jax.ShapeDtypeStruct((T // n_mem, MEM_WIDTH, n_mem), BF16),
                   jax.ShapeDtypeStruct((T, MEM_WIDTH), BF16)],
        compiler_params=_cparams(("parallel",), 32),
        name="mem_kv",
    )(mem, gain.reshape(1, D), w, k_norm.reshape(1, MEM_HEAD_DIM))


def _mem_attn_step_kernel(q_ref, mk_ref, mv_ref, qn_ref, o_ref):
    for i in range(q_ref.shape[0]):
        q = _rms(q_ref[i], qn_ref[...])
        s = jnp.sum(mk_ref[i] * q, axis=-1, keepdims=True) * (MEM_HEAD_DIM ** -0.5)
        e = jnp.exp(s - jnp.max(s, axis=0, keepdims=True))
        p = e / jnp.sum(e, axis=0, keepdims=True)
        o_ref[i] = jnp.sum(p * mv_ref[i], axis=0)


def _mem_attn_step(mq, cache_k, cache_v, layer, q_norm):
    nseq = mq.shape[0]
    n_mem = cache_k.shape[2]
    nb = 8
    qspec = pl.BlockSpec((nb, MEM_HEADS, MEM_HEAD_DIM), lambda i: (i, 0, 0))
    kvspec = pl.BlockSpec((None, nb, n_mem, MEM_HEADS, MEM_HEAD_DIM), lambda i: (layer, i, 0, 0, 0))
    out = pl.pallas_call(
        _mem_attn_step_kernel,
        grid=(nseq // nb,),
        in_specs=[qspec, kvspec, kvspec, pl.BlockSpec((1, MEM_HEAD_DIM), lambda i: (0, 0))],
        out_specs=qspec,
        out_shape=jax.ShapeDtypeStruct((nseq, MEM_HEADS, MEM_HEAD_DIM), F32),
        compiler_params=_cparams(("parallel",), 32),
        name="mem_attn_step",
    )(mq.reshape(nseq, MEM_HEADS, MEM_HEAD_DIM), cache_k, cache_v, q_norm.reshape(1, MEM_HEAD_DIM))
    return out.reshape(nseq, MEM_WIDTH)


def _out_proj_kernel(x_ref, tok_ref, mo_ref, w_ref, g_ref, x1_ref, h_ref):
    tw = tok_ref.shape[1]
    y = _dot(tok_ref[...].astype(BF16), w_ref[:tw, :]) + _dot(mo_ref[...].astype(BF16), w_ref[tw:, :])
    x1 = x_ref[...] + y
    x1_ref[...] = x1
    h_ref[...] = _rms(x1, g_ref[...]).astype(BF16)


def _out_proj(x, tok, mo, w, gain):
    T, D = x.shape
    tm = min(T, 256)
    const = lambda i: (0, 0)
    rows = lambda i: (i, 0)
    return pl.pallas_call(
        _out_proj_kernel,
        grid=(T // tm,),
        in_specs=[pl.BlockSpec((tm, D), rows), pl.BlockSpec((tm, tok.shape[1]), rows),
                  pl.BlockSpec((tm, mo.shape[1]), rows), pl.BlockSpec(w.shape, const),
                  pl.BlockSpec((1, D), const)],
        out_specs=[pl.BlockSpec((tm, D), rows)] * 2,
        out_shape=[jax.ShapeDtypeStruct((T, D), F32), jax.ShapeDtypeStruct((T, D), BF16)],
        compiler_params=_cparams(("parallel",), 32),
        name="out_proj",
    )(x, tok, mo, w, gain.reshape(1, D))


def _tail_prompt_kernel(pt_ref, bias_ref, x_ref, tok_ref, mq_ref, mkt_ref, mv_ref, qn_ref, wo_ref,
                        g_ref, wu_ref, cw_ref, cb_ref, wd_ref, qb_ref, kt_hbm, vt_hbm,
                        o_ref, cs_ref, dec_ref,
                        carry_scr, h_scr, act_scr, kbuf, vbuf, sem, dacc_scr, drun_scr,
                        *, seq_base, steps_per_seq, sb_layer):
    i = pl.program_id(1)
    nb = pl.num_programs(1)
    step = pl.program_id(0) * nb + i
    nsteps = pl.num_programs(0) * nb
    CH = DECODE_CHUNK
    NS = DECODE_SLOTS
    H = SB_HEADS
    P = kbuf.shape[-1]
    chunks_per_seq = pt_ref.shape[1] // CH
    cps = chunks_per_seq // steps_per_seq
    part = step % steps_per_seq

    def chunk_copies(g, slot):
        sq = seq_base + g // chunks_per_seq
        c = g % chunks_per_seq
        copies = []
        for p in range(CH):
            page = pt_ref[sq, (chunks_per_seq - 1 - c) * CH + p]
            copies.append(pltpu.make_async_copy(kt_hbm.at[sb_layer, page], kbuf.at[slot, p],
                                                sem.at[slot]))
            copies.append(pltpu.make_async_copy(vt_hbm.at[sb_layer, page], vbuf.at[slot, p],
                                                sem.at[slot]))
        return copies

    @pl.when(step == 0)
    def _():
        dacc_scr[...] = jnp.zeros_like(dacc_scr)
        drun_scr[...] = jnp.zeros_like(drun_scr)
        for g in range(NS - 1):
            for cp in chunk_copies(g, g):
                cp.start()

    @pl.when(i == 0)
    def _():
        carry_scr[...] = jnp.zeros_like(carry_scr)

    sub = lax.broadcasted_iota(jnp.int32, (H, P), 0)
    dbias = jnp.zeros((H, P), F32)
    for hh in range(H):
        dbias = jnp.where(sub == hh, bias_ref[hh], dbias)
    prow = lax.broadcasted_iota(jnp.int32, (P, P), 0)
    pcol = lax.broadcasted_iota(jnp.int32, (P, P), 1)
    later = jnp.where(prow > pcol, 1.0, 0.0).astype(BF16)

    def decode_chunk(j):
        g = step * cps + j
        slot = j % NS
        ahead = g + NS - 1
        if j + NS - 1 < cps:
            for cp in chunk_copies(ahead, (j + NS - 1) % NS):
                cp.start()
        else:
            @pl.when(ahead < nsteps * cps)
            def _():
                for cp in chunk_copies(ahead, (j + NS - 1) % NS):
                    cp.start()
        for cp in chunk_copies(g, slot):
            cp.wait()
        zs = []
        for p in range(CH):
            zp = dbias
            for hh in range(H):
                zh = jnp.sum(kbuf[slot, p, hh] * qb_ref[0, hh], axis=0, keepdims=True)
                zp = jnp.where(sub == hh, zh * (SB_HEAD_DIM ** -0.5) + dbias, zp)
            zs.append(zp)
        z = jnp.concatenate(zs, axis=0)
        nlf = _softplus(z)
        after = _dot(nlf.astype(BF16), later)
        run = drun_scr[...]
        ws = [None] * CH
        for p in reversed(range(CH)):
            rs = slice(p * H, (p + 1) * H)
            ws[p] = jnp.exp((z[rs] - nlf[rs]) - (after[rs] + run))
            run = run + jnp.sum(nlf[rs], axis=1, keepdims=True)
        drun_scr[...] = run
        for hh in range(H):
            a = dacc_scr[hh]
            for p in range(CH):
                a = a + vbuf[slot, p, hh] * ws[p][hh:hh + 1, :]
            dacc_scr[hh] = a

    tm = x_ref.shape[0]
    tw = tok_ref.shape[1]
    FC = FFN_CHUNK
    nch = D_FF // FC
    n_down = 2 if cps <= nch else 3
    n_attn = max(1, cps - nch - n_down)
    n_mid = cps - n_attn - n_down
    attn_before = {(m * MEM_HEADS) // n_attn: m for m in range(n_attn)}
    ffn_before = {(m * nch) // n_mid: n_attn + m for m in range(n_mid)}
    y = _dot(tok_ref[...].astype(BF16), wo_ref[:tw, :])
    for hd in range(MEM_HEADS):
        if hd in attn_before:
            decode_chunk(attn_before[hd])
        hs = slice(hd * MEM_HEAD_DIM, (hd + 1) * MEM_HEAD_DIM)
        qh = _rms(mq_ref[:, hs], qn_ref[...]).astype(BF16)
        sc = _dot(qh, mkt_ref[0, hs, :]) * (MEM_HEAD_DIM ** -0.5)
        e = jnp.exp(sc - jnp.max(sc, axis=-1, keepdims=True))
        o = _dot(e.astype(BF16), mv_ref[:, hs]) / jnp.sum(e, axis=-1, keepdims=True)
        y = y + _dot(o.astype(BF16), wo_ref[tw + hd * MEM_HEAD_DIM:tw + (hd + 1) * MEM_HEAD_DIM, :])
    x1 = x_ref[...] + y
    o_ref[...] = x1
    h_scr[...] = _rms(x1, g_ref[...]).astype(BF16)

    h = h_scr[...]
    rowi = lax.broadcasted_iota(jnp.int32, (tm, FC), 0)
    for c in range(nch):
        if c in ffn_before:
            decode_chunk(ffn_before[c])
        cs = slice(c * FC, (c + 1) * FC)
        gate = _dot(h, wu_ref[:, cs])
        val = _dot(h, wu_ref[:, D_FF + c * FC:D_FF + (c + 1) * FC])
        p0 = carry_scr[0:1, cs]
        p1 = carry_scr[1:2, cs]
        g1 = jnp.where(rowi == 0, p1, pltpu.roll(gate, 1, 0))
        g2 = jnp.where(rowi == 0, p0, jnp.where(rowi == 1, p1, pltpu.roll(gate, 2, 0)))
        a = cb_ref[:, cs] + g2 * cw_ref[0:1, cs] + g1 * cw_ref[1:2, cs] + gate * cw_ref[2:3, cs]
        act_scr[:, cs] = (_silu(a) * val).astype(BF16)
        last = gate[tm - (FFN_CONV - 1):, :]
        carry_scr[0:FFN_CONV - 1, cs] = last
        cs_ref[0, :, cs] = last
    for m in range(n_down):
        decode_chunk(n_attn + n_mid + m)
        ks = slice(((m * nch) // n_down) * FC, (((m + 1) * nch) // n_down) * FC)
        o_ref[...] += _dot(act_scr[:, ks], wd_ref[ks, :])

    @pl.when(part == steps_per_seq - 1)
    def _():
        ones = jnp.ones((H, P), BF16)
        subd = lax.broadcasted_iota(jnp.int32, (H, SB_HEAD_DIM), 0)
        out = jnp.zeros((H, SB_HEAD_DIM), F32)
        for hh in range(H):
            hi, lo = _split_bf16(dacc_scr[hh])
            out = jnp.where(subd == hh, _dot_nt(ones, hi) + _dot_nt(ones, lo), out)
        dec_ref[0] = out
        dacc_scr[...] = jnp.zeros_like(dacc_scr)
        drun_scr[...] = jnp.zeros_like(drun_scr)


def _tail_prompt(x, tok, mq, mkt, mvb, q_norm, w_o, gain, w_up, conv_w, conv_b, w_down, layer,
                 batch, seq, dec_q, cache_kt, cache_vt, sb_layer, page_table, sb_bias, seq_base,
                 n_dec):
    T, D = x.shape
    n_mem = mkt.shape[2]
    tm = 512
    nb = seq // tm
    steps_per_seq = (batch * nb) // n_dec
    n_pages = page_table.shape[1]
    P = cache_kt.shape[-1]
    CH = DECODE_CHUNK
    assert steps_per_seq * n_dec == batch * nb
    NS = DECODE_SLOTS
    cps = n_pages // (CH * steps_per_seq)
    assert cps * CH * steps_per_seq == n_pages and cps % NS == 0
    assert 4 <= cps <= D_FF // FFN_CHUNK + 3 + MEM_HEADS
    this_layer = lambda b, i, pt: (layer, 0, 0)
    rows = lambda b, i, pt: (b * nb + i, 0)
    const = lambda b, i, pt: (0, 0)
    once = pl.Buffered(1)
    grid_spec = pltpu.PrefetchScalarGridSpec(
        num_scalar_prefetch=1,
        grid=(batch, nb),
        in_specs=[pl.BlockSpec(memory_space=pltpu.SMEM),
                  pl.BlockSpec((tm, D), rows), pl.BlockSpec((tm, tok.shape[1]), rows),
                  pl.BlockSpec((tm, MEM_WIDTH), rows),
                  pl.BlockSpec((1, MEM_WIDTH, n_mem), lambda b, i, pt: (b, 0, 0)),
                  pl.BlockSpec((n_mem, MEM_WIDTH), lambda b, i, pt: (b, 0)),
                  pl.BlockSpec((1, MEM_HEAD_DIM), const),
                  pl.BlockSpec(w_o.shape, const, pipeline_mode=once),
                  pl.BlockSpec((1, D), const),
                  pl.BlockSpec((None,) + w_up.shape[1:], this_layer, pipeline_mode=once),
                  pl.BlockSpec(conv_w.shape, const), pl.BlockSpec((1, D_FF), const),
                  pl.BlockSpec((None,) + w_down.shape[1:], this_layer, pipeline_mode=once),
                  pl.BlockSpec((1, SB_HEADS, SB_HEAD_DIM, P),
                               lambda b, i, pt: (seq_base + (b * nb + i) // steps_per_seq, 0, 0, 0)),
                  pl.BlockSpec(memory_space=pl.ANY), pl.BlockSpec(memory_space=pl.ANY)],
        out_specs=[pl.BlockSpec((tm, D), rows),
                   pl.BlockSpec((1, FFN_CONV - 1, D_FF), lambda b, i, pt: (b, 0, 0)),
                   pl.BlockSpec((1, SB_HEADS, SB_HEAD_DIM),
                                lambda b, i, pt: ((b * nb + i) // steps_per_seq, 0, 0))],
        scratch_shapes=[pltpu.VMEM((8, D_FF), F32), pltpu.VMEM((tm, D), BF16),
                        pltpu.VMEM((tm, D_FF), BF16),
                        pltpu.VMEM((NS, CH, SB_HEADS, SB_HEAD_DIM, P), F32),
                        pltpu.VMEM((NS, CH, SB_HEADS, SB_HEAD_DIM, P), F32),
                        pltpu.SemaphoreType.DMA((NS,)),
                        pltpu.VMEM((SB_HEADS, SB_HEAD_DIM, P), F32),
                        pltpu.VMEM((SB_HEADS, P), F32)],
    )
    return pl.pallas_call(
        functools.partial(_tail_prompt_kernel, seq_base=seq_base, steps_per_seq=steps_per_seq,
                          sb_layer=sb_layer),
        grid_spec=grid_spec,
        out_shape=[jax.ShapeDtypeStruct((T, D), F32),
                   jax.ShapeDtypeStruct((batch, FFN_CONV - 1, D_FF), F32),
                   jax.ShapeDtypeStruct((n_dec, SB_HEADS, SB_HEAD_DIM), F32)],
        compiler_params=_cparams(("arbitrary", "arbitrary"), 60),
        name="tail_prompt",
    )(page_table, sb_bias, x, tok, mq, mkt, mvb, q_norm.reshape(1, MEM_HEAD_DIM), w_o,
      gain.reshape(1, D), w_up, conv_w, conv_b.reshape(1, D_FF), w_down, dec_q, cache_kt, cache_vt)


def _ffn_step_kernel(h_ref, x_ref, wg_ref, wv_ref, cw_ref, cb_ref, wd_ref, p0_ref, p1_ref,
                     o_ref, gate_ref):
    @pl.when(pl.program_id(0) == 0)
    def _():
        o_ref[...] = x_ref[...]

    h = h_ref[...]
    gate = _dot(h, wg_ref[...])
    val = _dot(h, wv_ref[...])
    a = (cb_ref[...] + p0_ref[...] * cw_ref[0:1, :] + p1_ref[...] * cw_ref[1:2, :]
         + gate * cw_ref[2:3, :])
    o_ref[...] += _dot((_silu(a) * val).astype(BF16), wd_ref[...])
    gate_ref[...] = gate


def _ffn_step(h, x1, w_up, conv_w, conv_b, w_down, layer, prev):
    T, D = x1.shape
    FC = FFN_CHUNK
    nch = D_FF // FC
    const = lambda c: (0, 0)
    chunk = lambda c: (0, c)
    out, gate = pl.pallas_call(
        _ffn_step_kernel,
        grid=(nch,),
        in_specs=[pl.BlockSpec((T, D), const), pl.BlockSpec((T, D), const),
                  pl.BlockSpec((None, D, FC), lambda c: (layer, 0, c)),
                  pl.BlockSpec((None, D, FC), lambda c: (layer, 0, nch + c)),
                  pl.BlockSpec((FFN_CONV, FC), chunk), pl.BlockSpec((1, FC), chunk),
                  pl.BlockSpec((None, FC, D), lambda c: (layer, c, 0)),
                  pl.BlockSpec((T, FC), chunk), pl.BlockSpec((T, FC), chunk)],
        out_specs=[pl.BlockSpec((T, D), const), pl.BlockSpec((T, FC), chunk)],
        out_shape=[jax.ShapeDtypeStruct((T, D), F32), jax.ShapeDtypeStruct((T, D_FF), F32)],
        compiler_params=_cparams(("arbitrary",), 32),
        name="ffn_step",
    )(h, x1, w_up, w_up, conv_w, conv_b.reshape(1, D_FF), w_down, prev[:, 0], prev[:, 1])
    return out, jnp.stack([prev[:, 1], gate], axis=1)


def kernel(x_prompt, x_sample, state_gla, cache_sb_k, cache_sb_v, cache_mem_k, cache_mem_v, state_ffn_conv, page_table, mem_prompt, norm_mix, w_in_gla, w_gate_up, b_gate, gla_out_norm, w_in_sb, sb_bias, mem_norm, w_mem_kv, mem_q_norm, mem_k_norm, w_out, norm_ffn, w_ffn_up, ffn_conv_w, ffn_conv_b, w_ffn_down):
    batch, seq, d_model = x_prompt.shape
    nseq = x_sample.shape[0]
    n_mem = mem_prompt.shape[1]
    depth = norm_mix.shape[0]
    gd0 = 2 * GLA_QK + 2 * GLA_VW
    gla_widths = (GLA_QK, GLA_QK, GLA_VW, GLA_VW, MEM_WIDTH)
    sb_widths = (SB_W, SB_W, SB_W, MEM_WIDTH)

    def gla_weights(j):
        w = w_in_gla[j]
        w_main = jnp.concatenate([w[:, :gd0], w[:, gd0 + GLA_GATE_RANK:]], axis=1).astype(BF16)
        w_gd = jnp.pad(w[:, gd0:gd0 + GLA_GATE_RANK], ((0, 0), (0, LANES - GLA_GATE_RANK))).astype(BF16)
        w_gu = jnp.pad(w_gate_up[j], ((0, LANES - GLA_GATE_RANK), (0, 0))).astype(BF16)
        return w_main, (w_gd, w_gu, b_gate[j])

    per_layer_bf16 = lambda w: [w[i].astype(BF16) for i in range(w.shape[0])]
    w_out_b = per_layer_bf16(w_out)
    w_up_b = w_ffn_up.astype(BF16)
    w_down_b = w_ffn_down.astype(BF16)
    w_kv_b = per_layer_bf16(w_mem_kv)
    w_sb_b = per_layer_bf16(w_in_sb)
    gla_w = [gla_weights(j) for j in range(w_in_gla.shape[0])]

    assert depth == 2 and x_sample.shape[1] == 1 and cache_sb_k.shape[0] == 1
    dec_seq = x_sample.shape[1]
    P = cache_sb_k.shape[2]

    def sample_front(x, i):
        j = i // 2
        if i % 2 == 0:
            w_main, gate = gla_w[j]
            q, k, v, g, mq, la = _proj(x, norm_mix[i], w_main, gla_widths, gate)
            tok, s_new = _gla_step(q, k, la, v, g, state_gla[j], gla_out_norm[j])
            return tok, mq, s_new
        q, k, v, mq = _proj(x, norm_mix[i], w_sb_b[j], sb_widths)
        return q, mq, (k, v)

    def sample_back(x, tok, mq, i):
        mo = _mem_attn_step(mq, cache_mem_k, cache_mem_v, i, mem_q_norm[i])
        x1, h2 = _out_proj(x, tok, mo, w_out_b[i], norm_ffn[i])
        return _ffn_step(h2, x1, w_up_b, ffn_conv_w[i], ffn_conv_b[i], w_down_b, i,
                         state_ffn_conv[i])

    xs = x_sample.reshape(nseq * dec_seq, d_model)
    tok0, mq0, s_gla0 = sample_front(xs, 0)
    xs, conv_s0 = sample_back(xs, tok0, mq0, 0)
    q_sb, mq1, (k_sb, v_sb) = sample_front(xs, 1)
    cache_kt = jnp.transpose(cache_sb_k, (0, 1, 3, 4, 2))
    cache_vt = jnp.transpose(cache_sb_v, (0, 1, 3, 4, 2))
    dec_q = jnp.broadcast_to(q_sb.reshape(nseq, SB_HEADS, SB_HEAD_DIM, 1),
                             (nseq, SB_HEADS, SB_HEAD_DIM, P))
    n_dec = nseq // depth
    dec_out = []

    x = x_prompt.reshape(batch * seq, d_model)
    mem = mem_prompt.reshape(batch * n_mem, d_model)
    p_gla, p_sbk, p_sbv, p_mk, p_mv, p_conv = [], [], [], [], [], []
    for i in range(depth):
        j = i // 2
        if i % 2 == 0:
            w_main, gate = gla_w[j]
            q, k, v, g, mq, la = _proj(x, norm_mix[i], w_main, gla_widths, gate)
            tok, s_fin = _gla_prompt(q, k, la, v, g, gla_out_norm[j], batch, seq)
            p_gla.append(s_fin)
        else:
            q, kt, v, vt, mq = _proj_sb_prompt(x, norm_mix[i], w_sb_b[j], batch, seq)
            tok = _sb_prompt(q, kt, v, sb_bias[j], batch, seq)
            by_pos = lambda t: t.reshape(batch, SB_HEADS, SB_HEAD_DIM, seq).transpose(0, 3, 1, 2)
            p_sbk.append(by_pos(kt))
            p_sbv.append(by_pos(vt))
        mk, mv, mkt, mvb = _mem_kv(mem, mem_norm[i], w_kv_b[i], mem_k_norm[i], n_mem)
        p_mk.append(mk.reshape(batch, n_mem, MEM_HEADS, MEM_HEAD_DIM))
        p_mv.append(mv.reshape(batch, n_mem, MEM_HEADS, MEM_HEAD_DIM))
        x, conv, dec = _tail_prompt(x, tok, mq, mkt, mvb, mem_q_norm[i], w_out_b[i], norm_ffn[i],
                                    w_up_b, ffn_conv_w[i], ffn_conv_b[i], w_down_b, i, batch, seq,
                                    dec_q, cache_kt, cache_vt, 0, page_table, sb_bias[0],
                                    i * n_dec, n_dec)
        p_conv.append(conv)
        dec_out.append(dec)
    y_prompt = x.reshape(batch, seq, d_model)

    tok1 = jnp.concatenate(dec_out, axis=0).reshape(nseq, SB_W)
    xs, conv_s1 = sample_back(xs, tok1, mq1, 1)
    y_sample = xs.reshape(nseq, dec_seq, d_model)
    s_gla = [s_gla0]
    s_sbk = [k_sb.reshape(nseq, dec_seq, SB_HEADS, SB_HEAD_DIM)]
    s_sbv = [v_sb.reshape(nseq, dec_seq, SB_HEADS, SB_HEAD_DIM)]
    s_conv = [conv_s0, conv_s1]

    return (y_prompt, y_sample,
            jnp.stack(p_gla), jnp.stack(s_gla),
            jnp.stack(p_sbk), jnp.stack(p_sbv), jnp.stack(s_sbk), jnp.stack(s_sbv),
            jnp.stack(p_mk), jnp.stack(p_mv),
            jnp.stack(p_conv), jnp.stack(s_conv))
```

```python
import functools

import jax
import jax.numpy as jnp
from jax import lax
from jax.experimental import pallas as pl
from jax.experimental.pallas import tpu as pltpu

F32 = jnp.float32
BF16 = jnp.bfloat16

GLA_HEADS = 4
GLA_DK = 64
GLA_DV = 128
GLA_GATE_RANK = 16
GLA_TAU = 16.0
GLA_CHUNK = 64
SB_HEADS = 8
SB_HEAD_DIM = 64
SB_QBLOCK = 128
MEM_HEADS = 4
MEM_HEAD_DIM = 128
D_FF = 2816
FFN_CONV = 3
EPS = 1e-6

GLA_QK = GLA_HEADS * GLA_DK
GLA_VW = GLA_HEADS * GLA_DV
SB_W = SB_HEADS * SB_HEAD_DIM
MEM_WIDTH = MEM_HEADS * MEM_HEAD_DIM

LOG2E = 1.4426950408889634
LANES = 128
FFN_CHUNK = 256
SB_PROMPT_QBLOCK = 2 * SB_QBLOCK
DECODE_CHUNK = 16
DECODE_SLOTS = 2


def _cparams(semantics, vmem_mib):
    return pltpu.CompilerParams(dimension_semantics=semantics,
                                vmem_limit_bytes=vmem_mib << 20)


def _dot(a, b):
    return jnp.dot(a, b, preferred_element_type=F32)


def _dot_nt(a, b):
    return lax.dot_general(a, b, (((1,), (1,)), ((), ())), preferred_element_type=F32)


def _rms(xf, g):
    ms = jnp.mean(xf * xf, axis=-1, keepdims=True)
    return xf * lax.rsqrt(ms + EPS) * g


def _log_sigmoid_pair(z):
    t = jnp.log1p(jnp.exp(-jnp.abs(z)))
    return jnp.minimum(z, 0.0) - t, jnp.minimum(-z, 0.0) - t


def _softplus(z):
    return jnp.maximum(z, 0.0) + jnp.log(1.0 + jnp.exp2(jnp.abs(z) * (-LOG2E)))


def _silu(x):
    return x * jax.nn.sigmoid(x)


def _split_bf16(x):
    hi = x.astype(BF16)
    lo = (x - hi.astype(F32)).astype(BF16)
    return hi, lo


def _dot_split(x, m):
    hi, lo = _split_bf16(x)
    return _dot(hi, m) + _dot(lo, m)


def _dot_split_left(m, x):
    hi, lo = _split_bf16(x)
    return _dot(m, hi) + _dot(m, lo)


def _proj_kernel(*refs, widths, gla):
    if gla:
        x_ref, g_ref, w_ref, wgd_ref, wgu_ref, bg_ref = refs[:6]
        outs = refs[6:]
    else:
        x_ref, g_ref, w_ref = refs[:3]
        outs = refs[3:]
    h = _rms(x_ref[...], g_ref[...]).astype(BF16)
    z = _dot(h, w_ref[...])
    off = 0
    for o_ref, wd in zip(outs, widths):
        o_ref[...] = z[:, off:off + wd]
        off += wd
    if gla:
        gd = _dot(h, wgd_ref[...])
        pre = _dot(gd.astype(BF16), wgu_ref[...]) + bg_ref[...]
        ls, _ = _log_sigmoid_pair(pre)
        outs[-1][...] = ls * (1.0 / GLA_TAU)


def _proj(x, gain, w, widths, gate=None):
    T, D = x.shape
    tm = min(T, 512)
    const = lambda i: (0, 0)
    in_specs = [pl.BlockSpec((tm, D), lambda i: (i, 0)),
                pl.BlockSpec((1, D), const),
                pl.BlockSpec(w.shape, const)]
    args = [x, gain.reshape(1, D), w]
    out_widths = list(widths)
    if gate is not None:
        wgd, wgu, bg = gate
        in_specs += [pl.BlockSpec(wgd.shape, const), pl.BlockSpec(wgu.shape, const),
                     pl.BlockSpec((1, GLA_QK), const)]
        args += [wgd, wgu, bg.reshape(1, GLA_QK)]
        out_widths.append(GLA_QK)
    return pl.pallas_call(
        functools.partial(_proj_kernel, widths=tuple(widths), gla=gate is not None),
        grid=(T // tm,),
        in_specs=in_specs,
        out_specs=[pl.BlockSpec((tm, wd), lambda i: (i, 0)) for wd in out_widths],
        out_shape=[jax.ShapeDtypeStruct((T, wd), F32) for wd in out_widths],
        compiler_params=_cparams(("parallel",), 40),
        name="proj_gla" if gate is not None else "proj_sb",
    )(*args)


def _proj_sb_prompt_kernel(x_ref, g_ref, w_ref, q_ref, kt_ref, v_ref, vt_ref, mq_ref):
    h = _rms(x_ref[...], g_ref[...]).astype(BF16)
    z = _dot(h, w_ref[...])
    q_ref[...] = z[:, :SB_W]
    k = z[:, SB_W:2 * SB_W]
    v = z[:, 2 * SB_W:3 * SB_W]
    kt_ref[0] = k.T
    v_ref[...] = v
    vt_ref[0] = v.T
    mq_ref[...] = z[:, 3 * SB_W:]


def _proj_sb_prompt(x, gain, w, batch, seq):
    T, D = x.shape
    tm = 512
    nb = seq // tm
    const = lambda i: (0, 0)
    rows = lambda i: (i, 0)
    cols = lambda i: (i // nb, 0, i % nb)
    row_spec = pl.BlockSpec((tm, SB_W), rows)
    col_spec = pl.BlockSpec((1, SB_W, tm), cols)
    row_shape = jax.ShapeDtypeStruct((T, SB_W), F32)
    col_shape = jax.ShapeDtypeStruct((batch, SB_W, seq), F32)
    return pl.pallas_call(
        _proj_sb_prompt_kernel,
        grid=(T // tm,),
        in_specs=[pl.BlockSpec((tm, D), rows), pl.BlockSpec((1, D), const),
                  pl.BlockSpec(w.shape, const)],
        out_specs=[row_spec, col_spec, row_spec, col_spec, row_spec],
        out_shape=[row_shape, col_shape, row_shape, col_shape, row_shape],
        compiler_params=_cparams(("parallel",), 48),
        name="proj_sb_prompt",
    )(x, gain.reshape(1, D), w)


def _gla_prompt_kernel(q_ref, k_ref, la_ref, v_ref, g_ref, on_ref, tok_ref, s_ref, s_scr):
    i = pl.program_id(1)

    @pl.when(i == 0)
    def _():
        s_scr[...] = jnp.zeros_like(s_scr)

    R = q_ref.shape[0]
    C = GLA_CHUNK
    row = lax.broadcasted_iota(jnp.int32, (R, R), 0)
    col = lax.broadcasted_iota(jnp.int32, (R, R), 1)
    same = (row // C) == (col // C)
    causal = jnp.logical_and(same, col <= row)
    ltri = jnp.where(causal, 1.0, 0.0).astype(BF16)
    ones_bd = jnp.where(same, 1.0, 0.0).astype(BF16)

    la = la_ref[...]
    b = _dot_split_left(ltri, la)
    bt = _dot_split_left(ones_bd, la)
    q_dec = q_ref[...] * (GLA_DK ** -0.5) * jnp.exp(b)
    k = k_ref[...]
    k_dec = k * jnp.exp(-b)
    kte_t = (k * jnp.exp(bt - b)).T
    dec_t = jnp.exp(bt).T

    for h in range(GLA_HEADS):
        ks = slice(h * GLA_DK, (h + 1) * GLA_DK)
        vs = slice(h * GLA_DV, (h + 1) * GLA_DV)
        qh = q_dec[:, ks].astype(BF16)
        kh = k_dec[:, ks].astype(BF16)
        att = jnp.where(causal, _dot_nt(qh, kh), 0.0)
        vh = v_ref[:, vs].astype(BF16)
        o = _dot(att.astype(BF16), vh)
        s = s_scr[h]
        inter = []
        for c in range(R // C):
            rs = slice(c * C, (c + 1) * C)
            inter.append(_dot(qh[rs], s.astype(BF16)))
            kv = _dot(kte_t[ks, rs].astype(BF16), vh[rs])
            s = dec_t[ks, c * C:c * C + 1] * s + kv
        s_scr[h] = s
        o = o + jnp.concatenate(inter, axis=0)
        o = _rms(o, on_ref[...])
        tok_ref[:, vs] = o * _silu(g_ref[:, vs])

    @pl.when(i == pl.num_programs(1) - 1)
    def _():
        s_ref[0] = s_scr[...]


def _gla_prompt(q, k, la, v, g, out_norm, batch, seq):
    R = 256
    nb = seq // R
    rows = lambda b, i: (b * nb + i, 0)
    return pl.pallas_call(
        _gla_prompt_kernel,
        grid=(batch, nb),
        in_specs=[pl.BlockSpec((R, GLA_QK), rows), pl.BlockSpec((R, GLA_QK), rows),
                  pl.BlockSpec((R, GLA_QK), rows), pl.BlockSpec((R, GLA_VW), rows),
                  pl.BlockSpec((R, GLA_VW), rows),
                  pl.BlockSpec((1, GLA_DV), lambda b, i: (0, 0))],
        out_specs=[pl.BlockSpec((R, GLA_VW), rows),
                   pl.BlockSpec((1, GLA_HEADS, GLA_DK, GLA_DV), lambda b, i: (b, 0, 0, 0))],
        out_shape=[jax.ShapeDtypeStruct((batch * seq, GLA_VW), F32),
                   jax.ShapeDtypeStruct((batch, GLA_HEADS, GLA_DK, GLA_DV), F32)],
        scratch_shapes=[pltpu.VMEM((GLA_HEADS, GLA_DK, GLA_DV), F32)],
        compiler_params=_cparams(("parallel", "arbitrary"), 32),
        name="gla_prompt",
    )(q, k, la, v, g, out_norm.reshape(1, GLA_DV))


def _gla_step_kernel(qt_ref, kt_ref, lat_ref, v_ref, g_ref, s0_ref, on_ref, tok_ref, sn_ref):
    nb = v_ref.shape[0]
    for i in range(nb):
        for h in range(GLA_HEADS):
            vs = slice(h * GLA_DV, (h + 1) * GLA_DV)
            qc = qt_ref[i, :, h:h + 1] * (GLA_DK ** -0.5)
            kc = kt_ref[i, :, h:h + 1]
            ac = jnp.exp(lat_ref[i, :, h:h + 1])
            s = ac * s0_ref[i, h] + kc * v_ref[i:i + 1, vs]
            sn_ref[i, h] = s
            o = jnp.sum(qc * s, axis=0, keepdims=True)
            o = _rms(o, on_ref[...])
            tok_ref[i:i + 1, vs] = o * _silu(g_ref[i:i + 1, vs])


def _gla_step(q, k, la, v, g, s0, out_norm):
    nseq = q.shape[0]
    nb = 8
    cols = lambda a: a.reshape(nseq, GLA_HEADS, GLA_DK).transpose(0, 2, 1)
    col_spec = pl.BlockSpec((nb, GLA_DK, GLA_HEADS), lambda i: (i, 0, 0))
    row_spec = pl.BlockSpec((nb, GLA_VW), lambda i: (i, 0))
    st_spec = pl.BlockSpec((nb, GLA_HEADS, GLA_DK, GLA_DV), lambda i: (i, 0, 0, 0))
    return pl.pallas_call(
        _gla_step_kernel,
        grid=(nseq // nb,),
        in_specs=[col_spec, col_spec, col_spec, row_spec, row_spec, st_spec,
                  pl.BlockSpec((1, GLA_DV), lambda i: (0, 0))],
        out_specs=[row_spec, st_spec],
        out_shape=[jax.ShapeDtypeStruct((nseq, GLA_VW), F32),
                   jax.ShapeDtypeStruct(s0.shape, F32)],
        compiler_params=_cparams(("parallel",), 32),
        name="gla_step",
    )(cols(q), cols(k), cols(la), v, g, s0, out_norm.reshape(1, GLA_DV))


def _sb_prompt_kernel(bias_ref, q_ref, kt_ref, v_ref, o_ref, qq_scr, c_scr, acc_scr, *, nq):
    hp = pl.program_id(1)
    i = pl.program_id(2)
    QB = SB_PROMPT_QBLOCK
    lane = lax.broadcasted_iota(jnp.int32, (QB, LANES), 1)
    first = lane < SB_HEAD_DIM
    rowi = lax.broadcasted_iota(jnp.int32, (2 * QB, QB), 0)
    coli = lax.broadcasted_iota(jnp.int32, (2 * QB, QB), 1)
    r1 = lax.broadcasted_iota(jnp.int32, (QB, QB), 0)
    c1 = lax.broadcasted_iota(jnp.int32, (QB, QB), 1)
    later = jnp.where(r1 > c1, 1.0, 0.0).astype(BF16)
    strict = coli < jnp.bitwise_and(rowi, QB - 1)
    b0 = bias_ref[2 * hp]
    b1 = bias_ref[2 * hp + 1]

    def block(kb, qq, c, masked):
        start = pl.multiple_of(kb * QB, QB)
        kk = kt_ref[0, :, pl.ds(start, QB)].astype(BF16)
        vv = v_ref[pl.ds(start, QB), :].astype(BF16)
        z = _dot(qq, kk)
        z = jnp.concatenate([z[:QB] + b0, z[QB:] + b1], axis=0)
        nlf = _softplus(z)
        if masked:
            nlf = jnp.where(strict, nlf, 0.0)
        after = _dot(nlf.astype(BF16), later) + jnp.concatenate([c] * (QB // LANES), axis=1)
        w = jnp.exp((z - nlf) - after)
        if masked:
            w = jnp.where(strict, w, 0.0)
        return c + jnp.sum(nlf, axis=1, keepdims=True), _dot(w.astype(BF16), vv)

    qblocks = (i, nq - 1 - i)
    for sel, qi in enumerate(qblocks):
        q = q_ref[pl.ds(pl.multiple_of(qi * QB, QB), QB), :] * (SB_HEAD_DIM ** -0.5)
        qq = jnp.concatenate([jnp.where(first, q, 0.0), jnp.where(first, 0.0, q)],
                             axis=0).astype(BF16)
        qq_scr[sel] = qq
        c, pv = block(qi, qq, jnp.zeros((2 * QB, LANES), F32), True)
        c_scr[sel] = c
        acc_scr[sel] = pv

    for t in range(nq - 1):
        sel = jnp.where(t < i, 0, 1)
        kb = jnp.where(t < i, i - 1 - t, nq - 2 - t)
        c, pv = block(kb, qq_scr[sel], c_scr[sel], False)
        c_scr[sel] = c
        acc_scr[sel] += pv

    for sel, qi in enumerate(qblocks):
        o_ref[pl.ds(pl.multiple_of(qi * QB, QB), QB), :] = jnp.where(
            first, acc_scr[sel, :QB], acc_scr[sel, QB:])


def _sb_prompt(q, kt, v, bias, batch, seq):
    QB = SB_PROMPT_QBLOCK
    nq = seq // QB
    npair = SB_W // LANES
    rowspec = pl.BlockSpec((seq, LANES), lambda b, hp, i: (b, hp))
    ktspec = pl.BlockSpec((1, LANES, seq), lambda b, hp, i: (b, hp, 0))
    return pl.pallas_call(
        functools.partial(_sb_prompt_kernel, nq=nq),
        grid=(batch, npair, nq // 2),
        in_specs=[pl.BlockSpec(memory_space=pltpu.SMEM), rowspec, ktspec, rowspec],
        out_specs=rowspec,
        out_shape=jax.ShapeDtypeStruct((batch * seq, SB_W), F32),
        scratch_shapes=[pltpu.VMEM((2, 2 * QB, LANES), BF16), pltpu.VMEM((2, 2 * QB, LANES), F32),
                        pltpu.VMEM((2, 2 * QB, LANES), F32)],
        compiler_params=_cparams(("parallel", "parallel", "arbitrary"), 32),
        name="sb_prompt",
    )(bias, q, kt, v)


def _mem_kv_kernel(m_ref, g_ref, w_ref, kn_ref, mk_ref, mv_ref, mkt_ref, mvb_ref):
    h = _rms(m_ref[...], g_ref[...]).astype(BF16)
    kv = _dot(h, w_ref[...])
    for hd in range(MEM_HEADS):
        hs = slice(hd * MEM_HEAD_DIM, (hd + 1) * MEM_HEAD_DIM)
        mk = _rms(kv[:, hs], kn_ref[...])
        mk_ref[:, hs] = mk
        mkt_ref[0, hs, :] = mk.T.astype(BF16)
    mv = kv[:, MEM_WIDTH:]
    mv_ref[...] = mv
    mvb_ref[...] = mv.astype(BF16)


def _mem_kv(mem, gain, w, k_norm, n_mem):
    T, D = mem.shape
    const = lambda i: (0, 0)
    rows = lambda i: (i, 0)
    return pl.pallas_call(
        _mem_kv_kernel,
        grid=(T // n_mem,),
        in_specs=[pl.BlockSpec((n_mem, D), rows), pl.BlockSpec((1, D), const),
                  pl.BlockSpec(w.shape, const), pl.BlockSpec((1, MEM_HEAD_DIM), const)],
        out_specs=[pl.BlockSpec((n_mem, MEM_WIDTH), rows), pl.BlockSpec((n_mem, MEM_WIDTH), rows),
                   pl.BlockSpec((1, MEM_WIDTH, n_mem), lambda i: (i, 0, 0)),
                   pl.BlockSpec((n_mem, MEM_WIDTH), rows)],
        out_shape=[jax.ShapeDtypeStruct((T, MEM_WIDTH), F32), jax.ShapeDtypeStruct((T, MEM_WIDTH), F32),
                   jax.ShapeDtypeStruct((T // n_mem, MEM_WIDTH, n_mem), BF16),
                   jax.ShapeDtypeStruct((T, MEM_WIDTH), BF16)],
        compiler_params=_cparams(("parallel",), 32),
        name="mem_kv",
    )(mem, gain.reshape(1, D), w, k_norm.reshape(1, MEM_HEAD_DIM))


def _mem_attn_step_kernel(q_ref, mk_ref, mv_ref, qn_ref, o_ref):
    for i in range(q_ref.shape[0]):
        q = _rms(q_ref[i], qn_ref[...])
        s = jnp.sum(mk_ref[i] * q, axis=-1, keepdims=True) * (MEM_HEAD_DIM ** -0.5)
        e = jnp.exp(s - jnp.max(s, axis=0, keepdims=True))
        p = e / jnp.sum(e, axis=0, keepdims=True)
        o_ref[i] = jnp.sum(p * mv_ref[i], axis=0)


def _mem_attn_step(mq, cache_k, cache_v, layer, q_norm):
    nseq = mq.shape[0]
    n_mem = cache_k.shape[2]
    nb = 8
    qspec = pl.BlockSpec((nb, MEM_HEADS, MEM_HEAD_DIM), lambda i: (i, 0, 0))
    kvspec = pl.BlockSpec((None, nb, n_mem, MEM_HEADS, MEM_HEAD_DIM), lambda i: (layer, i, 0, 0, 0))
    out = pl.pallas_call(
        _mem_attn_step_kernel,
        grid=(nseq // nb,),
        in_specs=[qspec, kvspec, kvspec, pl.BlockSpec((1, MEM_HEAD_DIM), lambda i: (0, 0))],
        out_specs=qspec,
        out_shape=jax.ShapeDtypeStruct((nseq, MEM_HEADS, MEM_HEAD_DIM), F32),
        compiler_params=_cparams(("parallel",), 32),
        name="mem_attn_step",
    )(mq.reshape(nseq, MEM_HEADS, MEM_HEAD_DIM), cache_k, cache_v, q_norm.reshape(1, MEM_HEAD_DIM))
    return out.reshape(nseq, MEM_WIDTH)


def _out_proj_kernel(x_ref, tok_ref, mo_ref, w_ref, g_ref, x1_ref, h_ref):
    tw = tok_ref.shape[1]
    y = _dot(tok_ref[...].astype(BF16), w_ref[:tw, :]) + _dot(mo_ref[...].astype(BF16), w_ref[tw:, :])
    x1 = x_ref[...] + y
    x1_ref[...] = x1
    h_ref[...] = _rms(x1, g_ref[...]).astype(BF16)


def _out_proj(x, tok, mo, w, gain):
    T, D = x.shape
    tm = min(T, 256)
    const = lambda i: (0, 0)
    rows = lambda i: (i, 0)
    return pl.pallas_call(
        _out_proj_kernel,
        grid=(T // tm,),
        in_specs=[pl.BlockSpec((tm, D), rows), pl.BlockSpec((tm, tok.shape[1]), rows),
                  pl.BlockSpec((tm, mo.shape[1]), rows), pl.BlockSpec(w.shape, const),
                  pl.BlockSpec((1, D), const)],
        out_specs=[pl.BlockSpec((tm, D), rows)] * 2,
        out_shape=[jax.ShapeDtypeStruct((T, D), F32), jax.ShapeDtypeStruct((T, D), BF16)],
        compiler_params=_cparams(("parallel",), 32),
        name="out_proj",
    )(x, tok, mo, w, gain.reshape(1, D))


def _tail_prompt_kernel(pt_ref, bias_ref, x_ref, tok_ref, mq_ref, mkt_ref, mv_ref, qn_ref, wo_ref,
                        g_ref, wu_ref, cw_ref, cb_ref, wd_ref, qb_ref, kt_hbm, vt_hbm,
                        o_ref, cs_ref, dec_ref,
                        carry_scr, h_scr, act_scr, kbuf, vbuf, sem, dacc_scr, drun_scr,
                        *, seq_base, steps_per_seq, sb_layer):
    i = pl.program_id(1)
    nb = pl.num_programs(1)
    step = pl.program_id(0) * nb + i
    nsteps = pl.num_programs(0) * nb
    CH = DECODE_CHUNK
    NS = DECODE_SLOTS
    H = SB_HEADS
    P = kbuf.shape[-1]
    chunks_per_seq = pt_ref.shape[1] // CH
    cps = chunks_per_seq // steps_per_seq
    part = step % steps_per_seq

    def chunk_copies(g, slot):
        sq = seq_base + g // chunks_per_seq
        c = g % chunks_per_seq
        copies = []
        for p in range(CH):
            page = pt_ref[sq, (chunks_per_seq - 1 - c) * CH + p]
            copies.append(pltpu.make_async_copy(kt_hbm.at[sb_layer, page], kbuf.at[slot, p],
                                                sem.at[slot]))
            copies.append(pltpu.make_async_copy(vt_hbm.at[sb_layer, page], vbuf.at[slot, p],
                                                sem.at[slot]))
        return copies

    @pl.when(step == 0)
    def _():
        dacc_scr[...] = jnp.zeros_like(dacc_scr)
        drun_scr[...] = jnp.zeros_like(drun_scr)
        for g in range(NS - 1):
            for cp in chunk_copies(g, g):
                cp.start()

    @pl.when(i == 0)
    def _():
        carry_scr[...] = jnp.zeros_like(carry_scr)

    sub = lax.broadcasted_iota(jnp.int32, (H, P), 0)
    dbias = jnp.zeros((H, P), F32)
    for hh in range(H):
        dbias = jnp.where(sub == hh, bias_ref[hh], dbias)
    prow = lax.broadcasted_iota(jnp.int32, (P, P), 0)
    pcol = lax.broadcasted_iota(jnp.int32, (P, P), 1)
    later = jnp.where(prow > pcol, 1.0, 0.0).astype(BF16)

    def decode_chunk(j):
        g = step * cps + j
        slot = j % NS
        ahead = g + NS - 1
        if j + NS - 1 < cps:
            for cp in chunk_copies(ahead, (j + NS - 1) % NS):
                cp.start()
        else:
            @pl.when(ahead < nsteps * cps)
            def _():
                for cp in chunk_copies(ahead, (j + NS - 1) % NS):
                    cp.start()
        for cp in chunk_copies(g, slot):
            cp.wait()
        zs = []
        for p in range(CH):
            zp = dbias
            for hh in range(H):
                zh = jnp.sum(kbuf[slot, p, hh] * qb_ref[0, hh], axis=0, keepdims=True)
                zp = jnp.where(sub == hh, zh * (SB_HEAD_DIM ** -0.5) + dbias, zp)
            zs.append(zp)
        z = jnp.concatenate(zs, axis=0)
        nlf = _softplus(z)
        after = _dot(nlf.astype(BF16), later)
        run = drun_scr[...]
        ws = [None] * CH
        for p in reversed(range(CH)):
            rs = slice(p * H, (p + 1) * H)
            ws[p] = jnp.exp((z[rs] - nlf[rs]) - (after[rs] + run))
            run = run + jnp.sum(nlf[rs], axis=1, keepdims=True)
        drun_scr[...] = run
        for hh in range(H):
            a = dacc_scr[hh]
            for p in range(CH):
                a = a + vbuf[slot, p, hh] * ws[p][hh:hh + 1, :]
            dacc_scr[hh] = a

    tm = x_ref.shape[0]
    tw = tok_ref.shape[1]
    FC = FFN_CHUNK
    nch = D_FF // FC
    n_attn = 1
    n_mid = 0
    n_down = cps - n_attn
    attn_before = {0: 0}
    ffn_before = {}
    y = _dot(tok_ref[...].astype(BF16), wo_ref[:tw, :])
    for hd in range(MEM_HEADS):
        if hd in attn_before:
            decode_chunk(attn_before[hd])
        hs = slice(hd * MEM_HEAD_DIM, (hd + 1) * MEM_HEAD_DIM)
        qh = _rms(mq_ref[:, hs], qn_ref[...]).astype(BF16)
        sc = _dot(qh, mkt_ref[0, hs, :]) * (MEM_HEAD_DIM ** -0.5)
        e = jnp.exp(sc - jnp.max(sc, axis=-1, keepdims=True))
        o = _dot(e.astype(BF16), mv_ref[:, hs]) / jnp.sum(e, axis=-1, keepdims=True)
        y = y + _dot(o.astype(BF16), wo_ref[tw + hd * MEM_HEAD_DIM:tw + (hd + 1) * MEM_HEAD_DIM, :])
    x1 = x_ref[...] + y
    o_ref[...] = x1
    h_scr[...] = _rms(x1, g_ref[...]).astype(BF16)

    h = h_scr[...]
    rowi = lax.broadcasted_iota(jnp.int32, (tm, FC), 0)
    for c in range(nch):
        if c in ffn_before:
            decode_chunk(ffn_before[c])
        cs = slice(c * FC, (c + 1) * FC)
        gate = _dot(h, wu_ref[:, cs])
        val = _dot(h, wu_ref[:, D_FF + c * FC:D_FF + (c + 1) * FC])
        p0 = carry_scr[0:1, cs]
        p1 = carry_scr[1:2, cs]
        g1 = jnp.where(rowi == 0, p1, pltpu.roll(gate, 1, 0))
        g2 = jnp.where(rowi == 0, p0, jnp.where(rowi == 1, p1, pltpu.roll(gate, 2, 0)))
        a = cb_ref[:, cs] + g2 * cw_ref[0:1, cs] + g1 * cw_ref[1:2, cs] + gate * cw_ref[2:3, cs]
        act_scr[:, cs] = (_silu(a) * val).astype(BF16)
        last = gate[tm - (FFN_CONV - 1):, :]
        carry_scr[0:FFN_CONV - 1, cs] = last
        cs_ref[0, :, cs] = last
    for m in range(n_down):
        decode_chunk(n_attn + n_mid + m)
        ks = slice(((m * nch) // n_down) * FC, (((m + 1) * nch) // n_down) * FC)
        o_ref[...] += _dot(act_scr[:, ks], wd_ref[ks, :])

    @pl.when(part == steps_per_seq - 1)
    def _():
        ones = jnp.ones((H, P), BF16)
        subd = lax.broadcasted_iota(jnp.int32, (H, SB_HEAD_DIM), 0)
        out = jnp.zeros((H, SB_HEAD_DIM), F32)
        for hh in range(H):
            hi, lo = _split_bf16(dacc_scr[hh])
            out = jnp.where(subd == hh, _dot_nt(ones, hi) + _dot_nt(ones, lo), out)
        dec_ref[0] = out
        dacc_scr[...] = jnp.zeros_like(dacc_scr)
        drun_scr[...] = jnp.zeros_like(drun_scr)


def _tail_prompt(x, tok, mq, mkt, mvb, q_norm, w_o, gain, w_up, conv_w, conv_b, w_down, layer,
                 batch, seq, dec_q, cache_kt, cache_vt, sb_layer, page_table, sb_bias, seq_base,
                 n_dec):
    T, D = x.shape
    n_mem = mkt.shape[2]
    tm = 512
    nb = seq // tm
    steps_per_seq = (batch * nb) // n_dec
    n_pages = page_table.shape[1]
    P = cache_kt.shape[-1]
    CH = DECODE_CHUNK
    assert steps_per_seq * n_dec == batch * nb
    NS = DECODE_SLOTS
    cps = n_pages // (CH * steps_per_seq)
    assert cps * CH * steps_per_seq == n_pages and cps % NS == 0
    assert 2 <= cps <= 1 + D_FF // FFN_CHUNK
    this_layer = lambda b, i, pt: (layer, 0, 0)
    rows = lambda b, i, pt: (b * nb + i, 0)
    const = lambda b, i, pt: (0, 0)
    once = pl.Buffered(1)
    grid_spec = pltpu.PrefetchScalarGridSpec(
        num_scalar_prefetch=1,
        grid=(batch, nb),
        in_specs=[pl.BlockSpec(memory_space=pltpu.SMEM),
                  pl.BlockSpec((tm, D), rows), pl.BlockSpec((tm, tok.shape[1]), rows),
                  pl.BlockSpec((tm, MEM_WIDTH), rows),
                  pl.BlockSpec((1, MEM_WIDTH, n_mem), lambda b, i, pt: (b, 0, 0)),
                  pl.BlockSpec((n_mem, MEM_WIDTH), lambda b, i, pt: (b, 0)),
                  pl.BlockSpec((1, MEM_HEAD_DIM), const),
                  pl.BlockSpec(w_o.shape, const, pipeline_mode=once),
                  pl.BlockSpec((1, D), const),
                  pl.BlockSpec((None,) + w_up.shape[1:], this_layer, pipeline_mode=once),
                  pl.BlockSpec(conv_w.shape, const), pl.BlockSpec((1, D_FF), const),
                  pl.BlockSpec((None,) + w_down.shape[1:], this_layer, pipeline_mode=once),
                  pl.BlockSpec((1, SB_HEADS, SB_HEAD_DIM, P),
                               lambda b, i, pt: (seq_base + (b * nb + i) // steps_per_seq, 0, 0, 0)),
                  pl.BlockSpec(memory_space=pl.ANY), pl.BlockSpec(memory_space=pl.ANY)],
        out_specs=[pl.BlockSpec((tm, D), rows),
                   pl.BlockSpec((1, FFN_CONV - 1, D_FF), lambda b, i, pt: (b, 0, 0)),
                   pl.BlockSpec((1, SB_HEADS, SB_HEAD_DIM),
                                lambda b, i, pt: ((b * nb + i) // steps_per_seq, 0, 0))],
        scratch_shapes=[pltpu.VMEM((8, D_FF), F32), pltpu.VMEM((tm, D), BF16),
                        pltpu.VMEM((tm, D_FF), BF16),
                        pltpu.VMEM((NS, CH, SB_HEADS, SB_HEAD_DIM, P), F32),
                        pltpu.VMEM((NS, CH, SB_HEADS, SB_HEAD_DIM, P), F32),
                        pltpu.SemaphoreType.DMA((NS,)),
                        pltpu.VMEM((SB_HEADS, SB_HEAD_DIM, P), F32),
                        pltpu.VMEM((SB_HEADS, P), F32)],
    )
    return pl.pallas_call(
        functools.partial(_tail_prompt_kernel, seq_base=seq_base, steps_per_seq=steps_per_seq,
                          sb_layer=sb_layer),
        grid_spec=grid_spec,
        out_shape=[jax.ShapeDtypeStruct((T, D), F32),
                   jax.ShapeDtypeStruct((batch, FFN_CONV - 1, D_FF), F32),
                   jax.ShapeDtypeStruct((n_dec, SB_HEADS, SB_HEAD_DIM), F32)],
        compiler_params=_cparams(("arbitrary", "arbitrary"), 60),
        name="tail_prompt",
    )(page_table, sb_bias, x, tok, mq, mkt, mvb, q_norm.reshape(1, MEM_HEAD_DIM), w_o,
      gain.reshape(1, D), w_up, conv_w, conv_b.reshape(1, D_FF), w_down, dec_q, cache_kt, cache_vt)


def _ffn_step_kernel(h_ref, x_ref, wg_ref, wv_ref, cw_ref, cb_ref, wd_ref, p0_ref, p1_ref,
                     o_ref, gate_ref):
    @pl.when(pl.program_id(0) == 0)
    def _():
        o_ref[...] = x_ref[...]

    h = h_ref[...]
    gate = _dot(h, wg_ref[...])
    val = _dot(h, wv_ref[...])
    a = (cb_ref[...] + p0_ref[...] * cw_ref[0:1, :] + p1_ref[...] * cw_ref[1:2, :]
         + gate * cw_ref[2:3, :])
    o_ref[...] += _dot((_silu(a) * val).astype(BF16), wd_ref[...])
    gate_ref[...] = gate


def _ffn_step(h, x1, w_up, conv_w, conv_b, w_down, layer, prev):
    T, D = x1.shape
    FC = FFN_CHUNK
    nch = D_FF // FC
    const = lambda c: (0, 0)
    chunk = lambda c: (0, c)
    out, gate = pl.pallas_call(
        _ffn_step_kernel,
        grid=(nch,),
        in_specs=[pl.BlockSpec((T, D), const), pl.BlockSpec((T, D), const),
                  pl.BlockSpec((None, D, FC), lambda c: (layer, 0, c)),
                  pl.BlockSpec((None, D, FC), lambda c: (layer, 0, nch + c)),
                  pl.BlockSpec((FFN_CONV, FC), chunk), pl.BlockSpec((1, FC), chunk),
                  pl.BlockSpec((None, FC, D), lambda c: (layer, c, 0)),
                  pl.BlockSpec((T, FC), chunk), pl.BlockSpec((T, FC), chunk)],
        out_specs=[pl.BlockSpec((T, D), const), pl.BlockSpec((T, FC), chunk)],
        out_shape=[jax.ShapeDtypeStruct((T, D), F32), jax.ShapeDtypeStruct((T, D_FF), F32)],
        compiler_params=_cparams(("arbitrary",), 32),
        name="ffn_step",
    )(h, x1, w_up, w_up, conv_w, conv_b.reshape(1, D_FF), w_down, prev[:, 0], prev[:, 1])
    return out, jnp.stack([prev[:, 1], gate], axis=1)


def kernel(x_prompt, x_sample, state_gla, cache_sb_k, cache_sb_v, cache_mem_k, cache_mem_v, state_ffn_conv, page_table, mem_prompt, norm_mix, w_in_gla, w_gate_up, b_gate, gla_out_norm, w_in_sb, sb_bias, mem_norm, w_mem_kv, mem_q_norm, mem_k_norm, w_out, norm_ffn, w_ffn_up, ffn_conv_w, ffn_conv_b, w_ffn_down):
    batch, seq, d_model = x_prompt.shape
    nseq = x_sample.shape[0]
    n_mem = mem_prompt.shape[1]
    depth = norm_mix.shape[0]
    gd0 = 2 * GLA_QK + 2 * GLA_VW
    gla_widths = (GLA_QK, GLA_QK, GLA_VW, GLA_VW, MEM_WIDTH)
    sb_widths = (SB_W, SB_W, SB_W, MEM_WIDTH)

    def gla_weights(j):
        w = w_in_gla[j]
        w_main = jnp.concatenate([w[:, :gd0], w[:, gd0 + GLA_GATE_RANK:]], axis=1).astype(BF16)
        w_gd = jnp.pad(w[:, gd0:gd0 + GLA_GATE_RANK], ((0, 0), (0, LANES - GLA_GATE_RANK))).astype(BF16)
        w_gu = jnp.pad(w_gate_up[j], ((0, LANES - GLA_GATE_RANK), (0, 0))).astype(BF16)
        return w_main, (w_gd, w_gu, b_gate[j])

    per_layer_bf16 = lambda w: [w[i].astype(BF16) for i in range(w.shape[0])]
    w_out_b = per_layer_bf16(w_out)
    w_up_b = w_ffn_up.astype(BF16)
    w_down_b = w_ffn_down.astype(BF16)
    w_kv_b = per_layer_bf16(w_mem_kv)
    w_sb_b = per_layer_bf16(w_in_sb)
    gla_w = [gla_weights(j) for j in range(w_in_gla.shape[0])]

    assert depth == 2 and x_sample.shape[1] == 1 and cache_sb_k.shape[0] == 1
    dec_seq = x_sample.shape[1]
    P = cache_sb_k.shape[2]

    def sample_front(x, i):
        j = i // 2
        if i % 2 == 0:
            w_main, gate = gla_w[j]
            q, k, v, g, mq, la = _proj(x, norm_mix[i], w_main, gla_widths, gate)
            tok, s_new = _gla_step(q, k, la, v, g, state_gla[j], gla_out_norm[j])
            return tok, mq, s_new
        q, k, v, mq = _proj(x, norm_mix[i], w_sb_b[j], sb_widths)
        return q, mq, (k, v)

    def sample_back(x, tok, mq, i):
        mo = _mem_attn_step(mq, cache_mem_k, cache_mem_v, i, mem_q_norm[i])
        x1, h2 = _out_proj(x, tok, mo, w_out_b[i], norm_ffn[i])
        return _ffn_step(h2, x1, w_up_b, ffn_conv_w[i], ffn_conv_b[i], w_down_b, i,
                         state_ffn_conv[i])

    xs = x_sample.reshape(nseq * dec_seq, d_model)
    tok0, mq0, s_gla0 = sample_front(xs, 0)
    xs, conv_s0 = sample_back(xs, tok0, mq0, 0)
    q_sb, mq1, (k_sb, v_sb) = sample_front(xs, 1)
    cache_kt = jnp.transpose(cache_sb_k, (0, 1, 3, 4, 2))
    cache_vt = jnp.transpose(cache_sb_v, (0, 1, 3, 4, 2))
    dec_q = jnp.broadcast_to(q_sb.reshape(nseq, SB_HEADS, SB_HEAD_DIM, 1),
                             (nseq, SB_HEADS, SB_HEAD_DIM, P))
    n_dec = nseq // depth
    dec_out = []

    x = x_prompt.reshape(batch * seq, d_model)
    mem = mem_prompt.reshape(batch * n_mem, d_model)
    p_gla, p_sbk, p_sbv, p_mk, p_mv, p_conv = [], [], [], [], [], []
    for i in range(depth):
        j = i // 2
        if i % 2 == 0:
            w_main, gate = gla_w[j]
            q, k, v, g, mq, la = _proj(x, norm_mix[i], w_main, gla_widths, gate)
            tok, s_fin = _gla_prompt(q, k, la, v, g, gla_out_norm[j], batch, seq)
            p_gla.append(s_fin)
        else:
            q, kt, v, vt, mq = _proj_sb_prompt(x, norm_mix[i], w_sb_b[j], batch, seq)
            tok = _sb_prompt(q, kt, v, sb_bias[j], batch, seq)
            by_pos = lambda t: t.reshape(batch, SB_HEADS, SB_HEAD_DIM, seq).transpose(0, 3, 1, 2)
            p_sbk.append(by_pos(kt))
            p_sbv.append(by_pos(vt))
        mk, mv, mkt, mvb = _mem_kv(mem, mem_norm[i], w_kv_b[i], mem_k_norm[i], n_mem)
        p_mk.append(mk.reshape(batch, n_mem, MEM_HEADS, MEM_HEAD_DIM))
        p_mv.append(mv.reshape(batch, n_mem, MEM_HEADS, MEM_HEAD_DIM))
        x, conv, dec = _tail_prompt(x, tok, mq, mkt, mvb, mem_q_norm[i], w_out_b[i], norm_ffn[i],
                                    w_up_b, ffn_conv_w[i], ffn_conv_b[i], w_down_b, i, batch, seq,
                                    dec_q, cache_kt, cache_vt, 0, page_table, sb_bias[0],
                                    i * n_dec, n_dec)
        p_conv.append(conv)
        dec_out.append(dec)
    y_prompt = x.reshape(batch, seq, d_model)

    tok1 = jnp.concatenate(dec_out, axis=0).reshape(nseq, SB_W)
    xs, conv_s1 = sample_back(xs, tok1, mq1, 1)
    y_sample = xs.reshape(nseq, dec_seq, d_model)
    s_gla = [s_gla0]
    s_sbk = [k_sb.reshape(nseq, dec_seq, SB_HEADS, SB_HEAD_DIM)]
    s_sbv = [v_sb.reshape(nseq, dec_seq, SB_HEADS, SB_HEAD_DIM)]
    s_conv = [conv_s0, conv_s1]

    return (y_prompt, y_sample,
            jnp.stack(p_gla), jnp.stack(s_gla),
            jnp.stack(p_sbk), jnp.stack(p_sbv), jnp.stack(s_sbk), jnp.stack(s_sbv),
            jnp.stack(p_mk), jnp.stack(p_mv),
            jnp.stack(p_conv), jnp.stack(s_conv))
```
